```python
import jax, jax.numpy as jnp
from jax import lax
import numpy as np


D_MODEL = 1024
BATCH = 8
SEQ = 2048
DEPTH = 4

D_MIX = D_MODEL
D_NSA = D_MIX // 2
D_LRU = D_MIX - D_NSA
NSA_HEADS = 8
NSA_HEAD_DIM = D_NSA // NSA_HEADS
NSA_KV_GROUPS = 2
NSA_REP = NSA_HEADS // NSA_KV_GROUPS
NSA_KV_W = NSA_KV_GROUPS * NSA_HEAD_DIM
N_BRANCH = 3
CMP_LEN = 32
CMP_STRIDE = 16
CMP_HIDDEN = 4 * NSA_HEAD_DIM
SEL_BLOCK = 64
SEL_TOP_K = 8
WINDOW = 512
Q_BLOCK = 128
LRU_BLOCKS = 8
LRU_BLOCK_W = D_LRU // LRU_BLOCKS
LRU_CONV_W = 4
LRU_C = 8.0
D_FF = 2816
FFN_CONV_W = 3
NORM_EPS = 1e-6
NEG_INF = -1e30
FORCE_BONUS = 1e4
IN_COLS = D_NSA + 6 * NSA_KV_W + N_BRANCH * NSA_HEADS + 2 * D_LRU

kernel_name = 'hymba_nsa_rglru_convffn_adaln'


def _in_splits():
    widths = [D_NSA] + [NSA_KV_W] * 6 + [N_BRANCH * NSA_HEADS, D_LRU, D_LRU]
    return [int(v) for v in np.cumsum(widths)[:-1]]


def _rmsnorm(x, g):
    xf = x.astype(jnp.float32)
    y = xf * lax.rsqrt(jnp.mean(xf * xf, axis=-1, keepdims=True) + NORM_EPS)
    return y.astype(x.dtype) * g


def _masked_softmax(s, mask):
    s = jnp.where(mask, s.astype(jnp.float32), NEG_INF)
    return jnp.where(mask, jax.nn.softmax(s, axis=-1), 0.0)


def _causal_dwconv(x, w, b):
    width = w.shape[0]
    seq = x.shape[1]
    xp = jnp.pad(x, ((0, 0), (width - 1, 0), (0, 0)))
    y = b
    for k in range(width):
        y = y + xp[:, k:k + seq] * w[k]
    return y


def _split_q(z):
    B, S, _ = z.shape
    return z.reshape(B, S, NSA_KV_GROUPS, NSA_REP, NSA_HEAD_DIM).transpose(0, 2, 3, 1, 4)


def _split_kv(z):
    B, S, _ = z.shape
    return z.reshape(B, S, NSA_KV_GROUPS, NSA_HEAD_DIM).transpose(0, 2, 1, 3)


def _compress(k, pos, w1, b1, w2, b2):
    S = k.shape[2]
    nc = (S - CMP_LEN) // CMP_STRIDE + 1
    idx = np.arange(nc)[:, None] * CMP_STRIDE + np.arange(CMP_LEN)[None, :]
    blocks = k[:, :, idx] + pos
    flat = blocks.reshape(blocks.shape[0], blocks.shape[1], nc, CMP_LEN * NSA_HEAD_DIM)
    return jax.nn.gelu(flat @ w1 + b1) @ w2 + b2


def _nsa_mixer(zq, zkc, zvc, zks, zvs, zkw, zvw, zg, gate_b, cmp_pos, cmp_w1, cmp_b1, cmp_w2, cmp_b2):
    B, S, _ = zq.shape
    G, R, DH = NSA_KV_GROUPS, NSA_REP, NSA_HEAD_DIM
    scale = DH ** -0.5
    t = jnp.arange(S)
    q = _split_q(zq)

    kc = _compress(_split_kv(zkc), cmp_pos[0], cmp_w1[0], cmp_b1[0], cmp_w2[0], cmp_b2[0])
    vc = _compress(_split_kv(zvc), cmp_pos[1], cmp_w1[1], cmp_b1[1], cmp_w2[1], cmp_b2[1])
    nc = kc.shape[2]
    cmp_start = jnp.arange(nc) * CMP_STRIDE
    mask_c = (cmp_start + CMP_LEN - 1)[None, :] <= t[:, None]
    p_cmp = _masked_softmax(jnp.einsum('bgrsd,bgcd->bgrsc', q, kc) * scale, mask_c)
    o_cmp = jnp.einsum('bgrsc,bgcd->bgrsd', p_cmp.astype(vc.dtype), vc)

    nb = S // SEL_BLOCK
    js = jnp.arange(nb) * SEL_BLOCK
    overlap = ((cmp_start[:, None] < js[None, :] + SEL_BLOCK) & (cmp_start[:, None] + CMP_LEN > js[None, :])).astype(jnp.float32)
    imp = jnp.einsum('bgrsc,cj->bgsj', p_cmp, overlap)
    qblk = t // SEL_BLOCK
    jb = jnp.arange(nb)[None, :]
    valid = jb <= qblk[:, None]
    forced = (jb == 0) | (jb == qblk[:, None]) | (jb == qblk[:, None] - 1)
    imp = jnp.where(valid, imp + jnp.where(forced, FORCE_BONUS, 0.0), NEG_INF)
    top_k = min(SEL_TOP_K, nb)
    _, sel = lax.top_k(imp, top_k)

    nq = S // Q_BLOCK
    ks_b = _split_kv(zks).reshape(B, G, nb, SEL_BLOCK, DH)
    vs_b = _split_kv(zvs).reshape(B, G, nb, SEL_BLOCK, DH)
    q_blk = q.reshape(B, G, R, nq, Q_BLOCK, DH).transpose(3, 0, 1, 2, 4, 5)
    sel_blk = sel.reshape(B, G, nq, Q_BLOCK, top_k).transpose(2, 0, 1, 3, 4)
    t_blk = t.reshape(nq, Q_BLOCK)
    bi = jnp.arange(B)[:, None, None, None]
    gi = jnp.arange(G)[None, :, None, None]
    n_sel = top_k * SEL_BLOCK

    def sel_attend(args):
        qb, ib, tb = args
        kg = ks_b[bi, gi, ib].reshape(B, G, Q_BLOCK, n_sel, DH)
        vg = vs_b[bi, gi, ib].reshape(B, G, Q_BLOCK, n_sel, DH)
        kpos = (ib[..., None] * SEL_BLOCK + jnp.arange(SEL_BLOCK)).reshape(B, G, Q_BLOCK, n_sel)
        mask = (kpos <= tb[:, None])[:, :, None]
        p = _masked_softmax(jnp.einsum('bgrqd,bgqnd->bgrqn', qb, kg) * scale, mask)
        return jnp.einsum('bgrqn,bgqnd->bgrqd', p.astype(vg.dtype), vg)

    o_slc = lax.map(sel_attend, (q_blk, sel_blk, t_blk))
    o_slc = o_slc.transpose(1, 2, 3, 0, 4, 5).reshape(B, G, R, S, DH)

    nw = WINDOW // Q_BLOCK + 1
    kp = jnp.pad(_split_kv(zkw), ((0, 0), (0, 0), (WINDOW, 0), (0, 0))).reshape(B, G, nq + nw - 1, Q_BLOCK, DH)
    vp = jnp.pad(_split_kv(zvw), ((0, 0), (0, 0), (WINDOW, 0), (0, 0))).reshape(B, G, nq + nw - 1, Q_BLOCK, DH)
    kband = jnp.concatenate([kp[:, :, i:i + nq] for i in range(nw)], axis=3)
    vband = jnp.concatenate([vp[:, :, i:i + nq] for i in range(nw)], axis=3)
    qw = q.reshape(B, G, R, nq, Q_BLOCK, DH)
    kpos = jnp.arange(nq)[:, None] * Q_BLOCK - WINDOW + jnp.arange(nw * Q_BLOCK)[None, :]
    kpos = kpos[:, None, :]
    tq = t_blk[:, :, None]
    mask_w = (kpos <= tq) & (kpos > tq - WINDOW) & (kpos >= 0)
    p_w = _masked_softmax(jnp.einsum('bgrnqd,bgnkd->bgrnqk', qw, kband) * scale, mask_w)
    o_win = jnp.einsum('bgrnqk,bgnkd->bgrnqd', p_w.astype(vband.dtype), vband).reshape(B, G, R, S, DH)

    g = jax.nn.sigmoid(zg + gate_b).reshape(B, S, G, R, N_BRANCH).transpose(0, 2, 3, 1, 4)
    o = g[..., 0:1] * o_cmp + g[..., 1:2] * o_slc + g[..., 2:3] * o_win
    return o.transpose(0, 3, 1, 2, 4).reshape(B, S, D_NSA)


def _lin_combine(c1, c2):
    a1, b1 = c1
    a2, b2 = c2
    return a1 * a2, a2 * b1 + b2


def _rglru_mixer(zx, zy, conv_w, conv_b, wa, ba, wx, bx, lam):
    B, S, _ = zx.shape
    xc = _causal_dwconv(zx, conv_w, conv_b)
    xh = xc.reshape(B, S, LRU_BLOCKS, LRU_BLOCK_W)
    r = jax.nn.sigmoid(jnp.einsum('bsnc,ncd->bsnd', xh, wa).reshape(B, S, D_LRU) + ba)
    i = jax.nn.sigmoid(jnp.einsum('bsnc,ncd->bsnd', xh, wx).reshape(B, S, D_LRU) + bx)
    log_a = LRU_C * r.astype(jnp.float32) * jax.nn.log_sigmoid(lam.astype(jnp.float32))
    a = jnp.exp(log_a)
    u = jnp.sqrt(-jnp.expm1(2.0 * log_a)) * (i * xc).astype(jnp.float32)
    _, h = lax.associative_scan(_lin_combine, (a, u), axis=1)
    return h.astype(zx.dtype) * jax.nn.gelu(zy)


def _conv_ffn(h, w_gate, w_up, conv_w, conv_b, w_down):
    gate = _causal_dwconv(h @ w_gate, conv_w, conv_b)
    return (jax.nn.silu(gate) * (h @ w_up)) @ w_down


def setup_inputs(seed: int = 0) -> dict:
    key = jax.random.key(seed)
    ks = jax.random.split(key, 32)

    def nrm(k, shape, scale):
        return jax.random.normal(k, shape, jnp.float32) * scale

    L = DEPTH
    lam_u = jax.random.uniform(ks[19], (L, D_LRU), jnp.float32, 0.9, 0.999)
    return {
        'x': nrm(ks[0], (BATCH, SEQ, D_MODEL), 1.0),
        'c': nrm(ks[1], (BATCH, D_MODEL), 1.0),
        'ada_w': nrm(ks[2], (L, D_MODEL, 6 * D_MODEL), 0.5 * D_MODEL ** -0.5),
        'ada_b': nrm(ks[3], (L, 6 * D_MODEL), 0.02),
        'mix_norm_g': 1.0 + nrm(ks[4], (L, D_MODEL), 0.02),
        'ffn_norm_g': 1.0 + nrm(ks[5], (L, D_MODEL), 0.02),
        'w_in': nrm(ks[6], (L, D_MODEL, IN_COLS), D_MODEL ** -0.5),
        'nsa_gate_b': nrm(ks[7], (L, N_BRANCH * NSA_HEADS), 0.1),
        'cmp_pos': nrm(ks[8], (L, 2, CMP_LEN, NSA_HEAD_DIM), 0.02),
        'cmp_w1': nrm(ks[9], (L, 2, CMP_LEN * NSA_HEAD_DIM, CMP_HIDDEN), (CMP_LEN * NSA_HEAD_DIM) ** -0.5),
        'cmp_b1': nrm(ks[10], (L, 2, CMP_HIDDEN), 0.01),
        'cmp_w2': nrm(ks[11], (L, 2, CMP_HIDDEN, NSA_HEAD_DIM), CMP_HIDDEN ** -0.5),
        'cmp_b2': nrm(ks[12], (L, 2, NSA_HEAD_DIM), 0.01),
        'lru_conv_w': nrm(ks[13], (L, LRU_CONV_W, D_LRU), LRU_CONV_W ** -0.5),
        'lru_conv_b': nrm(ks[14], (L, D_LRU), 0.01),
        'lru_wa': nrm(ks[15], (L, LRU_BLOCKS, LRU_BLOCK_W, LRU_BLOCK_W), LRU_BLOCK_W ** -0.5),
        'lru_ba': nrm(ks[16], (L, D_LRU), 0.01),
        'lru_wx': nrm(ks[17], (L, LRU_BLOCKS, LRU_BLOCK_W, LRU_BLOCK_W), LRU_BLOCK_W ** -0.5),
        'lru_bx': nrm(ks[18], (L, D_LRU), 0.01),
        'lru_lambda': jnp.log(lam_u) - jnp.log1p(-lam_u),
        'nsa_out_norm_g': 1.0 + nrm(ks[20], (L, D_NSA), 0.02),
        'lru_out_norm_g': 1.0 + nrm(ks[21], (L, D_LRU), 0.02),
        'w_out': nrm(ks[22], (L, D_MIX, D_MODEL), D_MIX ** -0.5),
        'ffn_w_gate': nrm(ks[23], (L, D_MODEL, D_FF), D_MODEL ** -0.5),
        'ffn_w_up': nrm(ks[24], (L, D_MODEL, D_FF), D_MODEL ** -0.5),
        'ffn_conv_w': nrm(ks[25], (L, FFN_CONV_W, D_FF), FFN_CONV_W ** -0.5),
        'ffn_conv_b': nrm(ks[26], (L, D_FF), 0.01),
        'ffn_w_down': nrm(ks[27], (L, D_FF, D_MODEL), D_FF ** -0.5),
        'final_norm_g': 1.0 + nrm(ks[28], (D_MODEL,), 0.02),
    }


def reference(x, c, ada_w, ada_b, mix_norm_g, ffn_norm_g, w_in, nsa_gate_b, cmp_pos, cmp_w1, cmp_b1, cmp_w2, cmp_b2, lru_conv_w, lru_conv_b, lru_wa, lru_ba, lru_wx, lru_bx, lru_lambda, nsa_out_norm_g, lru_out_norm_g, w_out, ffn_w_gate, ffn_w_up, ffn_conv_w, ffn_conv_b, ffn_w_down, final_norm_g):
    splits = _in_splits()
    c_act = jax.nn.silu(c)
    for l in range(DEPTH):
        mod = (c_act @ ada_w[l] + ada_b[l])[:, None, :]
        sh1, sc1, g1, sh2, sc2, g2 = jnp.split(mod, 6, axis=-1)

        h = _rmsnorm(x, mix_norm_g[l]) * (1.0 + sc1) + sh1
        z = h @ w_in[l]
        zq, zkc, zvc, zks, zvs, zkw, zvw, zg, zx, zy = jnp.split(z, splits, axis=-1)
        o_nsa = _nsa_mixer(zq, zkc, zvc, zks, zvs, zkw, zvw, zg, nsa_gate_b[l], cmp_pos[l], cmp_w1[l], cmp_b1[l], cmp_w2[l], cmp_b2[l])
        o_lru = _rglru_mixer(zx, zy, lru_conv_w[l], lru_conv_b[l], lru_wa[l], lru_ba[l], lru_wx[l], lru_bx[l], lru_lambda[l])
        mixed = jnp.concatenate([_rmsnorm(o_nsa, nsa_out_norm_g[l]), _rmsnorm(o_lru, lru_out_norm_g[l])], axis=-1)
        x = x + g1 * (mixed @ w_out[l])

        h = _rmsnorm(x, ffn_norm_g[l]) * (1.0 + sc2) + sh2
        x = x + g2 * _conv_ffn(h, ffn_w_gate[l], ffn_w_up[l], ffn_conv_w[l], ffn_conv_b[l], ffn_w_down[l])
    return _rmsnorm(x, final_norm_g)
```

```python
import functools

import numpy as np
import jax
import jax.numpy as jnp
from jax import lax
from jax.experimental import pallas as pl
from jax.experimental.pallas import tpu as pltpu

F32 = jnp.float32
BF16 = jnp.bfloat16

D_MODEL = 1024
DEPTH = 4
D_NSA = 512
D_LRU = 512
HEADS = 8
HEAD_DIM = 64
KV_GROUPS = 2
REP = HEADS // KV_GROUPS
N_BRANCH = 3
CMP_LEN = 32
CMP_STRIDE = 16
CMP_HIDDEN = 4 * HEAD_DIM
SEL_BLOCK = 64
SEL_TOP_K = 8
WINDOW = 512
Q_BLOCK = 128
LRU_BLOCKS = 8
LRU_BLOCK_W = D_LRU // LRU_BLOCKS
LRU_CONV_W = 4
LRU_C = 8.0
D_FF = 2816
FFN_CONV_W = 3
NORM_EPS = 1e-6
NEG_INF = -1e30
FORCE_BONUS = 1e4

KV_W = KV_GROUPS * HEAD_DIM
IN_COLS = D_NSA + 6 * KV_W + N_BRANCH * HEADS + 2 * D_LRU
LANES = 128
KEY_CHUNK = 256
ROW_TILE = 512
LRU_TILE = 256
FF_CHUNK = 256
HIST = 8

C_Q = 0
C_KC = C_Q + D_NSA
C_VC = C_KC + KV_W
C_KK = C_VC + KV_W
C_VV = C_KK + KV_GROUPS * LANES
C_ZX = C_VV + KV_GROUPS * LANES
C_ZY = C_ZX + D_LRU
C_GATE = C_ZY + D_LRU
IN_COLS_P = C_GATE + KV_GROUPS * LANES


def _dot(a, b):
    return jnp.dot(a, b, preferred_element_type=F32)


def _dot_nt(a, b):
    return lax.dot_general(a, b, (((1,), (1,)), ((), ())), preferred_element_type=F32)


def _sigmoid(x):
    return 1.0 / (1.0 + jnp.exp(-x))


def _gelu_tanh(x):
    c = np.float32(np.sqrt(2.0 / np.pi))
    return x * (0.5 * (1.0 + jnp.tanh(c * (x + 0.044715 * (x * x * x)))))


def _rms(x):
    return x * lax.rsqrt(jnp.mean(x * x, axis=-1, keepdims=True) + NORM_EPS)


def _mod_kernel(c_ref, w_ref, b_ref, o_ref):
    c = c_ref[...]
    ca = (c * _sigmoid(c)).astype(BF16)
    o_ref[...] = _dot(ca, w_ref[...].astype(BF16)) + b_ref[...]


def _mod_call(c, ada_w, ada_b):
    nb = c.shape[0]
    out = pl.pallas_call(
        _mod_kernel,
        grid=(DEPTH, 6),
        in_specs=[
            pl.BlockSpec((nb, D_MODEL), lambda l, j: (0, 0)),
            pl.BlockSpec((None, D_MODEL, D_MODEL), lambda l, j: (l, 0, j)),
            pl.BlockSpec((None, 1, D_MODEL), lambda l, j: (l, 0, j)),
        ],
        out_specs=pl.BlockSpec((None, nb, D_MODEL), lambda l, j: (l, 0, j)),
        out_shape=jax.ShapeDtypeStruct((DEPTH, nb, 6 * D_MODEL), F32),
        name="adaln_mod",
    )(c, ada_w, ada_b.reshape(DEPTH, 1, 6 * D_MODEL))
    return out.reshape(DEPTH, nb, 6, D_MODEL)


def _in_kernel(x_ref, mod_ref, g_ref, w_ref, gb_ref,
               q_ref, kc_ref, vc_ref, kk_ref, vv_ref, zx_ref, zy_ref, gate_ref):
    x = x_ref[0]
    h = (_rms(x) * g_ref[...]) * (1.0 + mod_ref[1:2, :]) + mod_ref[0:1, :]
    hb = h.astype(BF16)

    def seg(start, width):
        return _dot(hb, w_ref[:, start:start + width])

    q_ref[0] = (seg(C_Q, D_NSA) * (HEAD_DIM ** -0.5)).astype(BF16)
    kc_ref[0] = seg(C_KC, KV_W)
    vc_ref[0] = seg(C_VC, KV_W)
    for g in range(KV_GROUPS):
        kk_ref[0, g] = seg(C_KK + g * LANES, LANES).astype(BF16)
        vv_ref[0, g] = seg(C_VV + g * LANES, LANES).astype(BF16)
        gate_ref[0, g] = _sigmoid(seg(C_GATE + g * LANES, LANES) + gb_ref[:, g * LANES:(g + 1) * LANES])
    zx_ref[0] = seg(C_ZX, D_LRU)
    zy_ref[0] = seg(C_ZY, D_LRU)


def _in_call(x, mod, norm_g, w_in_p, gate_b_p, l):
    nb, seq, _ = x.shape
    tm = ROW_TILE
    row = lambda width: pl.BlockSpec((1, tm, width), lambda b, t: (b, t, 0))
    grp = pl.BlockSpec((1, KV_GROUPS, tm, LANES), lambda b, t: (b, 0, t, 0))
    return pl.pallas_call(
        _in_kernel,
        grid=(nb, seq // tm),
        in_specs=[
            row(D_MODEL),
            pl.BlockSpec((None, None, 6, D_MODEL), lambda b, t: (l, b, 0, 0)),
            pl.BlockSpec((None, 1, D_MODEL), lambda b, t: (l, 0, 0)),
            pl.BlockSpec((None, D_MODEL, IN_COLS_P), lambda b, t: (l, 0, 0)),
            pl.BlockSpec((None, 1, KV_GROUPS * LANES), lambda b, t: (l, 0, 0)),
        ],
        out_specs=[row(D_NSA), row(KV_W), row(KV_W), grp, grp, row(D_LRU), row(D_LRU), grp],
        out_shape=[
            jax.ShapeDtypeStruct((nb, seq, D_NSA), BF16),
            jax.ShapeDtypeStruct((nb, seq, KV_W), F32),
            jax.ShapeDtypeStruct((nb, seq, KV_W), F32),
            jax.ShapeDtypeStruct((nb, KV_GROUPS, seq, LANES), BF16),
            jax.ShapeDtypeStruct((nb, KV_GROUPS, seq, LANES), BF16),
            jax.ShapeDtypeStruct((nb, seq, D_LRU), F32),
            jax.ShapeDtypeStruct((nb, seq, D_LRU), F32),
            jax.ShapeDtypeStruct((nb, KV_GROUPS, seq, LANES), F32),
        ],
        compiler_params=pltpu.CompilerParams(
            dimension_semantics=("arbitrary", "arbitrary"), vmem_limit_bytes=48 * 1024 * 1024),
        name="mixer_in",
    )(x, mod, norm_g, w_in_p, gate_b_p)


def _cmp_kernel(kc_ref, vc_ref, pos_ref, w1_ref, b1_ref, w2_ref, b2_ref, o_ref):
    def hidden(x_ref, i):
        x = x_ref[0]
        top = (x + pos_ref[2 * i:2 * i + 1, :]).astype(BF16)
        bot = (x + pos_ref[2 * i + 1:2 * i + 2, :]).astype(BF16)
        a = _dot(top, w1_ref[2 * i])
        b = _dot(bot, w1_ref[2 * i + 1])
        nrow = b.shape[0]
        b = pltpu.roll(b, nrow - 1, axis=0)
        return _gelu_tanh(a + b + b1_ref[i:i + 1, :])

    hk = hidden(kc_ref, 0)
    hv = hidden(vc_ref, 1)
    for g in range(KV_GROUPS):
        sl = slice(g * CMP_HIDDEN, (g + 1) * CMP_HIDDEN)
        hg = jnp.concatenate([hk[:, sl], hv[:, sl]], axis=1).astype(BF16)
        o_ref[0, g] = (_dot(hg, w2_ref[...]) + b2_ref[...]).astype(BF16)


def _cmp_call(kc16, vc16, pos_p, w1_p, b1_p, w2_p, b2_p, l):
    nb, nrow, width = kc16.shape
    blk = pl.BlockSpec((1, nrow, width), lambda b: (b, 0, 0))
    return pl.pallas_call(
        _cmp_kernel,
        grid=(nb,),
        in_specs=[
            blk, blk,
            pl.BlockSpec((None, 4, width), lambda b: (l, 0, 0)),
            pl.BlockSpec((None, 4, width, KV_GROUPS * CMP_HIDDEN), lambda b: (l, 0, 0, 0)),
            pl.BlockSpec((None, 2, KV_GROUPS * CMP_HIDDEN), lambda b: (l, 0, 0)),
            pl.BlockSpec((None, 2 * CMP_HIDDEN, LANES), lambda b: (l, 0, 0)),
            pl.BlockSpec((None, 1, LANES), lambda b: (l, 0, 0)),
        ],
        out_specs=pl.BlockSpec((1, KV_GROUPS, nrow, LANES), lambda b: (b, 0, 0, 0)),
        out_shape=jax.ShapeDtypeStruct((nb, KV_GROUPS, nrow, LANES), BF16),
        compiler_params=pltpu.CompilerParams(
            dimension_semantics=("arbitrary",), vmem_limit_bytes=48 * 1024 * 1024),
        name="nsa_compress",
    )(kc16, vc16, pos_p, w1_p, b1_p, w2_p, b2_p)


def _attn_kernel(q_ref, kk_ref, vv_ref, kvc_ref, gate_ref, e_ref, ot_ref, o_ref,
                 m_ref, l_ref, acc_ref):
    qb = pl.program_id(2)
    nq = Q_BLOCK
    q = q_ref[0]
    qs = jnp.concatenate([q[:, r * HEAD_DIM:(r + 1) * HEAD_DIM] for r in range(REP)], axis=0)
    zpad = jnp.zeros_like(qs)
    q_lo = jnp.concatenate([qs, zpad], axis=1)
    q_hi = jnp.concatenate([zpad, qs], axis=1)
    trow = qb * nq + lax.broadcasted_iota(jnp.int32, (nq, KEY_CHUNK), 0)

    kvc = kvc_ref[0, 0]
    ncb = kvc.shape[0]
    sc = _dot_nt(q_lo, kvc).reshape(REP, nq, ncb)
    cidx = lax.broadcasted_iota(jnp.int32, (nq, ncb), 1)
    mask_c = (cidx * CMP_STRIDE + (CMP_LEN - 1)) <= trow[:, :ncb]
    sc = jnp.where(mask_c[None], sc, NEG_INF)
    mc = jnp.max(sc, axis=-1, keepdims=True)
    pc = jnp.where(mask_c[None], jnp.exp(sc - mc), 0.0)
    lc = jnp.sum(pc, axis=-1, keepdims=True)
    pc = pc / jnp.where(lc > 0.0, lc, 1.0)
    o_cmp = _dot(pc.reshape(REP * nq, ncb).astype(BF16), kvc)[:, HEAD_DIM:]

    psum = pc[0] + pc[1] + pc[2] + pc[3]
    p_hi = psum.astype(BF16)
    p_lo = (psum - p_hi.astype(F32)).astype(BF16)
    ot = ot_ref[...]
    imp = _dot_nt(ot, p_hi) + _dot_nt(ot, p_lo)
    nblk = imp.shape[0]
    jj = lax.broadcasted_iota(jnp.int32, (nblk, nq), 0)
    lane = lax.broadcasted_iota(jnp.int32, (nblk, nq), 1)
    qblk = jnp.right_shift(qb * nq + lane, int(np.log2(SEL_BLOCK)))
    valid = jj <= qblk
    forced = (jj == 0) | (jj == qblk) | (jj == qblk - 1)
    imp = jnp.where(valid, imp + jnp.where(forced, FORCE_BONUS, 0.0), NEG_INF)
    rank = jnp.zeros((nblk, nq), F32)
    for i in range(nblk):
        row = imp[i:i + 1, :]
        beats = (row > imp) | ((row == imp) & (jj > i))
        rank = rank + jnp.where(beats, 1.0, 0.0)
    sel_t = jnp.where(rank < float(SEL_TOP_K), 1.0, 0.0)
    sel_t = jnp.concatenate([sel_t, jnp.zeros((LANES - nblk, nq), F32)], axis=0)
    selq = sel_t.T.astype(BF16)

    def attend(q_pad, lo, hi, bias_fn):
        m_ref[...] = jnp.full(m_ref.shape, NEG_INF, F32)
        l_ref[...] = jnp.zeros(l_ref.shape, F32)
        acc_ref[...] = jnp.zeros(acc_ref.shape, F32)

        def body(c, carry):
            k0 = pl.multiple_of(c * KEY_CHUNK, KEY_CHUNK)
            kk = kk_ref[0, 0, pl.ds(k0, KEY_CHUNK), :]
            vv = vv_ref[0, 0, pl.ds(k0, KEY_CHUNK), :]
            kpos = k0 + lax.broadcasted_iota(jnp.int32, (nq, KEY_CHUNK), 1)
            bias = bias_fn(k0, kpos)
            s = _dot_nt(q_pad, kk).reshape(REP, nq, KEY_CHUNK) + bias[None]
            s = s.reshape(REP * nq, KEY_CHUNK)
            m_prev = m_ref[...]
            m_next = jnp.maximum(m_prev, jnp.max(s, axis=1, keepdims=True))
            alpha = jnp.exp(m_prev - m_next)
            p = jnp.exp(s - jnp.concatenate([m_next] * (KEY_CHUNK // LANES), axis=1))
            l_ref[...] = alpha * l_ref[...] + jnp.sum(p, axis=1, keepdims=True)
            acc_ref[...] = alpha * acc_ref[...] + _dot(p.astype(BF16), vv)
            m_ref[...] = m_next
            return carry

        lax.fori_loop(lo, hi, body, 0)
        return acc_ref[...] / l_ref[...]

    def sel_bias(k0, kpos):
        msk = _dot(selq, e_ref[:, pl.ds(k0, KEY_CHUNK)])
        return jnp.where((msk > 0.5) & (kpos <= trow), 0.0, NEG_INF)

    def win_bias(k0, kpos):
        return jnp.where((kpos <= trow) & (kpos > trow - WINDOW), 0.0, NEG_INF)

    hi = qb // 2 + 1
    o_slc = attend(q_lo, 0, hi, sel_bias)[:, :HEAD_DIM]
    o_win = attend(q_hi, jnp.maximum(qb - WINDOW // nq, 0) // 2, hi, win_bias)[:, HEAD_DIM:]

    gt = gate_ref[0, 0]

    def gcol(br):
        return jnp.concatenate(
            [gt[:, N_BRANCH * r + br:N_BRANCH * r + br + 1] for r in range(REP)], axis=0)

    o = gcol(0) * o_cmp + gcol(1) * o_slc + gcol(2) * o_win
    o_ref[0] = jnp.concatenate([o[r * nq:(r + 1) * nq] for r in range(REP)], axis=1)


def _attn_call(q, kk, vv, kvc, gate, expand, overlap_t):
    nb, seq, _ = q.shape
    nq = seq // Q_BLOCK
    ncb = kvc.shape[2]
    per_bg = lambda rows: pl.BlockSpec((1, 1, rows, LANES), lambda b, g, i: (b, g, 0, 0))
    return pl.pallas_call(
        _attn_kernel,
        grid=(nb, KV_GROUPS, nq),
        in_specs=[
            pl.BlockSpec((1, Q_BLOCK, REP * HEAD_DIM), lambda b, g, i: (b, i, g)),
            per_bg(seq), per_bg(seq), per_bg(ncb),
            pl.BlockSpec((1, 1, Q_BLOCK, LANES), lambda b, g, i: (b, g, i, 0)),
            pl.BlockSpec(expand.shape, lambda b, g, i: (0, 0)),
            pl.BlockSpec(overlap_t.shape, lambda b, g, i: (0, 0)),
        ],
        out_specs=pl.BlockSpec((1, Q_BLOCK, REP * HEAD_DIM), lambda b, g, i: (b, i, g)),
        out_shape=jax.ShapeDtypeStruct((nb, seq, D_NSA), F32),
        scratch_shapes=[
            pltpu.VMEM((REP * Q_BLOCK, LANES), F32),
            pltpu.VMEM((REP * Q_BLOCK, LANES), F32),
            pltpu.VMEM((REP * Q_BLOCK, LANES), F32),
        ],
        compiler_params=pltpu.CompilerParams(
            dimension_semantics=("arbitrary", "arbitrary", "arbitrary"),
            vmem_limit_bytes=48 * 1024 * 1024),
        name="nsa_attention",
    )(q, kk, vv, kvc, gate, expand, overlap_t)


def _lru_kernel(zx_ref, zy_ref, cw_ref, cb_ref, w_ref, bias_ref, lam_ref, o_ref, hist_ref, h_ref):
    t = pl.program_id(1)
    tl = zx_ref.shape[1]

    @pl.when(t == 0)
    def _():
        hist_ref[...] = jnp.zeros(hist_ref.shape, F32)
        h_ref[...] = jnp.zeros(h_ref.shape, F32)

    x = zx_ref[0]
    xe = jnp.concatenate([hist_ref[...], x], axis=0)
    hist_ref[...] = x[tl - HIST:, :]
    xc = cb_ref[...] + xe[HIST:, :] * cw_ref[LRU_CONV_W - 1:LRU_CONV_W, :]
    for k in range(LRU_CONV_W - 1):
        off = HIST - (LRU_CONV_W - 1) + k
        xc = xc + xe[off:off + tl, :] * cw_ref[k:k + 1, :]

    y = _dot(xc.astype(BF16), w_ref[...])
    r = _sigmoid(y[:, :D_LRU] + bias_ref[0:1, :])
    i = _sigmoid(y[:, D_LRU:] + bias_ref[1:2, :])
    lam = lam_ref[...]
    log_sig = -(jnp.maximum(-lam, 0.0) + jnp.log1p(jnp.exp(-jnp.abs(lam))))
    log_a = LRU_C * r * log_sig
    a = jnp.exp(log_a)
    u = jnp.sqrt(-jnp.tanh(log_a) * (a * a + 1.0)) * (i * xc)

    rows = lax.broadcasted_iota(jnp.int32, (tl, D_LRU), 0)
    d = 1
    while d < tl:
        ok = rows >= d
        a_sh = pltpu.roll(a, d, axis=0)
        u_sh = pltpu.roll(u, d, axis=0)
        u = jnp.where(ok, a * u_sh + u, u)
        a = jnp.where(ok, a * a_sh, a)
        d *= 2
    h = a * h_ref[HIST - 1:HIST, :] + u
    h_ref[...] = h[tl - HIST:, :]
    o_ref[0] = h * _gelu_tanh(zy_ref[0])


def _lru_call(zx, zy, conv_w, conv_b, w_ax, bias_ax, lam, l):
    nb, seq, _ = zx.shape
    tl = LRU_TILE
    row = pl.BlockSpec((1, tl, D_LRU), lambda b, t: (b, t, 0))
    return pl.pallas_call(
        _lru_kernel,
        grid=(nb, seq // tl),
        in_specs=[
            row, row,
            pl.BlockSpec((None, LRU_CONV_W, D_LRU), lambda b, t: (l, 0, 0)),
            pl.BlockSpec((None, 1, D_LRU), lambda b, t: (l, 0, 0)),
            pl.BlockSpec((None, D_LRU, 2 * D_LRU), lambda b, t: (l, 0, 0)),
            pl.BlockSpec((None, 2, D_LRU), lambda b, t: (l, 0, 0)),
            pl.BlockSpec((None, 1, D_LRU), lambda b, t: (l, 0, 0)),
        ],
        out_specs=row,
        out_shape=jax.ShapeDtypeStruct((nb, seq, D_LRU), F32),
        scratch_shapes=[pltpu.VMEM((HIST, D_LRU), F32), pltpu.VMEM((HIST, D_LRU), F32)],
        compiler_params=pltpu.CompilerParams(dimension_semantics=("arbitrary", "arbitrary")),
        name="rglru",
    )(zx, zy, conv_w, conv_b, w_ax, bias_ax, lam)


def _out_kernel(x_ref, on_ref, ol_ref, mod_ref, gn_ref, gl_ref, w_ref, o_ref):
    a = (_rms(on_ref[0]) * gn_ref[...]).astype(BF16)
    b = (_rms(ol_ref[0]) * gl_ref[...]).astype(BF16)
    y = _dot(a, w_ref[:D_NSA, :]) + _dot(b, w_ref[D_NSA:, :])
    o_ref[0] = x_ref[0] + mod_ref[2:3, :] * y


def _out_call(x, o_nsa, o_lru, mod, g_nsa, g_lru, w_out, l):
    nb, seq, _ = x.shape
    tm = ROW_TILE
    row = lambda width: pl.BlockSpec((1, tm, width), lambda b, t: (b, t, 0))
    return pl.pallas_call(
        _out_kernel,
        grid=(nb, seq // tm),
        in_specs=[
            row(D_MODEL), row(D_NSA), row(D_LRU),
            pl.BlockSpec((None, None, 6, D_MODEL), lambda b, t: (l, b, 0, 0)),
            pl.BlockSpec((None, 1, D_NSA), lambda b, t: (l, 0, 0)),
            pl.BlockSpec((None, 1, D_LRU), lambda b, t: (l, 0, 0)),
            pl.BlockSpec((None, D_MODEL, D_MODEL), lambda b, t: (l, 0, 0)),
        ],
        out_specs=row(D_MODEL),
        out_shape=jax.ShapeDtypeStruct(x.shape, F32),
        compiler_params=pltpu.CompilerParams(
            dimension_semantics=("arbitrary", "arbitrary"), vmem_limit_bytes=48 * 1024 * 1024),
        name="mixer_out",
    )(x, o_nsa, o_lru, mod, g_nsa, g_lru, w_out)


def _ffn_kernel(x_ref, mod_ref, g_ref, wg_ref, wu_ref, cw_ref, cb_ref, wd_ref, fg_ref, o_ref,
                hist_ref, acc_ref, *, final_norm):
    t = pl.program_id(1)
    tm = x_ref.shape[1]

    @pl.when(t == 0)
    def _():
        hist_ref[...] = jnp.zeros(hist_ref.shape, F32)

    x = x_ref[0]
    h = (_rms(x) * g_ref[...]) * (1.0 + mod_ref[4:5, :]) + mod_ref[3:4, :]
    hb = h.astype(BF16)
    for j in range(D_FF // FF_CHUNK):
        cs = slice(j * FF_CHUNK, (j + 1) * FF_CHUNK)
        gp = _dot(hb, wg_ref[:, cs])
        ge = jnp.concatenate([hist_ref[:, cs], gp], axis=0)
        hist_ref[:, cs] = gp[tm - HIST:, :]
        gate = cb_ref[:, cs] + gp * cw_ref[FFN_CONV_W - 1:FFN_CONV_W, cs]
        for k in range(FFN_CONV_W - 1):
            off = HIST - (FFN_CONV_W - 1) + k
            gate = gate + ge[off:off + tm, :] * cw_ref[k:k + 1, cs]
        act = (gate * _sigmoid(gate)) * _dot(hb, wu_ref[:, cs])
        contrib = _dot(act.astype(BF16), wd_ref[cs, :])
        if j == 0:
            acc_ref[...] = contrib
        else:
            acc_ref[...] += contrib
    y = x + mod_ref[5:6, :] * acc_ref[...]
    if final_norm:
        y = _rms(y) * fg_ref[...]
    o_ref[0] = y


def _ffn_call(x, mod, norm_g, w_gate, w_up, conv_w, conv_b, w_down, final_g, l, final_norm):
    nb, seq, _ = x.shape
    tm = ROW_TILE
    row = pl.BlockSpec((1, tm, D_MODEL), lambda b, t: (b, t, 0))
    once = dict(pipeline_mode=pl.Buffered(1))
    return pl.pallas_call(
        functools.partial(_ffn_kernel, final_norm=final_norm),
        grid=(nb, seq // tm),
        in_specs=[
            row,
            pl.BlockSpec((None, None, 6, D_MODEL), lambda b, t: (l, b, 0, 0)),
            pl.BlockSpec((None, 1, D_MODEL), lambda b, t: (l, 0, 0)),
            pl.BlockSpec((None, D_MODEL, D_FF), lambda b, t: (l, 0, 0), **once),
            pl.BlockSpec((None, D_MODEL, D_FF), lambda b, t: (l, 0, 0), **once),
            pl.BlockSpec((None, FFN_CONV_W, D_FF), lambda b, t: (l, 0, 0)),
            pl.BlockSpec((None, 1, D_FF), lambda b, t: (l, 0, 0)),
            pl.BlockSpec((None, D_FF, D_MODEL), lambda b, t: (l, 0, 0), **once),
            pl.BlockSpec((1, D_MODEL), lambda b, t: (0, 0)),
        ],
        out_specs=row,
        out_shape=jax.ShapeDtypeStruct(x.shape, F32),
        scratch_shapes=[pltpu.VMEM((HIST, D_FF), F32), pltpu.VMEM((tm, D_MODEL), F32)],
        compiler_params=pltpu.CompilerParams(
            dimension_semantics=("arbitrary", "arbitrary"), vmem_limit_bytes=56 * 1024 * 1024),
        name="conv_ffn",
    )(x, mod, norm_g, w_gate, w_up, conv_w, conv_b, w_down, final_g)


def _in_perm():
    zero = IN_COLS
    q0, kc0, vc0 = 0, D_NSA, D_NSA + KV_W
    ks0, vs0, kw0, vw0 = (D_NSA + i * KV_W for i in range(2, 6))
    zg0 = D_NSA + 6 * KV_W
    zx0 = zg0 + N_BRANCH * HEADS
    zy0 = zx0 + D_LRU
    cols = list(range(q0, q0 + D_NSA)) + list(range(kc0, kc0 + KV_W)) + list(range(vc0, vc0 + KV_W))
    for a0, b0 in ((ks0, kw0), (vs0, vw0)):
        for g in range(KV_GROUPS):
            cols += list(range(a0 + g * HEAD_DIM, a0 + (g + 1) * HEAD_DIM))
            cols += list(range(b0 + g * HEAD_DIM, b0 + (g + 1) * HEAD_DIM))
    cols += list(range(zx0, zx0 + D_LRU)) + list(range(zy0, zy0 + D_LRU))
    per_g = N_BRANCH * REP
    for g in range(KV_GROUPS):
        cols += list(range(zg0 + g * per_g, zg0 + (g + 1) * per_g)) + [zero] * (LANES - per_g)
    assert len(cols) == IN_COLS_P
    return np.asarray(cols, np.int32)


def _prep_in(w_in, gate_b):
    w_aug = jnp.concatenate([w_in, jnp.zeros(w_in.shape[:-1] + (1,), w_in.dtype)], axis=-1)
    w_p = jnp.take(w_aug, _in_perm(), axis=-1).astype(BF16)
    per_g = N_BRANCH * REP
    gb = gate_b.reshape(DEPTH, KV_GROUPS, per_g)
    gb = jnp.pad(gb, ((0, 0), (0, 0), (0, LANES - per_g))).reshape(DEPTH, 1, KV_GROUPS * LANES)
    return w_p, gb


def _prep_cmp(cmp_pos, cmp_w1, cmp_b1, cmp_w2, cmp_b2):
    half = CMP_LEN // 2
    pos = cmp_pos.reshape(DEPTH, 2, 2, half, 1, HEAD_DIM)
    pos = jnp.broadcast_to(pos, (DEPTH, 2, 2, half, KV_GROUPS, HEAD_DIM))
    pos_p = pos.reshape(DEPTH, 4, half * KV_W)
    w1 = cmp_w1.reshape(DEPTH, 2, 2, half, HEAD_DIM, CMP_HIDDEN)
    eye = jnp.eye(KV_GROUPS, dtype=w1.dtype)
    w1_p = jnp.einsum('lvhtdc,gG->lvhtgdGc', w1, eye)
    w1_p = w1_p.reshape(DEPTH, 4, half * KV_W, KV_GROUPS * CMP_HIDDEN).astype(BF16)
    b1_p = jnp.tile(cmp_b1, (1, 1, KV_GROUPS))
    zeros = jnp.zeros((DEPTH, CMP_HIDDEN, HEAD_DIM), cmp_w2.dtype)
    w2_p = jnp.concatenate([
        jnp.concatenate([cmp_w2[:, 0], zeros], axis=-1),
        jnp.concatenate([zeros, cmp_w2[:, 1]], axis=-1)], axis=1).astype(BF16)
    b2_p = cmp_b2.reshape(DEPTH, 1, 2 * HEAD_DIM)
    return pos_p, w1_p, b1_p, w2_p, b2_p


def _prep_lru(lru_wa, lru_wx, lru_ba, lru_bx):
    eye = jnp.eye(LRU_BLOCKS, dtype=lru_wa.dtype)
    dense = lambda w: jnp.einsum('lncd,nm->lncmd', w, eye).reshape(DEPTH, D_LRU, D_LRU)
    w_ax = jnp.concatenate([dense(lru_wa), dense(lru_wx)], axis=-1).astype(BF16)
    bias_ax = jnp.stack([lru_ba, lru_bx], axis=1)
    return w_ax, bias_ax


def _attn_consts(seq):
    nblk = seq // SEL_BLOCK
    ncb_p = seq // CMP_STRIDE
    expand = np.zeros((LANES, seq), np.float32)
    expand[np.arange(seq) // SEL_BLOCK, np.arange(seq)] = 1.0
    cstart = np.arange(ncb_p) * CMP_STRIDE
    js = np.arange(nblk) * SEL_BLOCK
    overlap_t = ((cstart[None, :] < js[:, None] + SEL_BLOCK) & (cstart[None, :] + CMP_LEN > js[:, None]))
    return jnp.asarray(expand, BF16), jnp.asarray(overlap_t.astype(np.float32), BF16)


def kernel(x, c, ada_w, ada_b, mix_norm_g, ffn_norm_g, w_in, nsa_gate_b, cmp_pos, cmp_w1, cmp_b1, cmp_w2, cmp_b2, lru_conv_w, lru_conv_b, lru_wa, lru_ba, lru_wx, lru_bx, lru_lambda, nsa_out_norm_g, lru_out_norm_g, w_out, ffn_w_gate, ffn_w_up, ffn_conv_w, ffn_conv_b, ffn_w_down, final_norm_g):
    nb, seq, _ = x.shape
    mod = _mod_call(c, ada_w, ada_b)
    w_in_p, gate_b_p = _prep_in(w_in, nsa_gate_b)
    pos_p, w1_p, b1_p, w2_p, b2_p = _prep_cmp(cmp_pos, cmp_w1, cmp_b1, cmp_w2, cmp_b2)
    w_ax, bias_ax = _prep_lru(lru_wa, lru_wx, lru_ba, lru_bx)
    expand, overlap_t = _attn_consts(seq)
    w_out_b = w_out.astype(BF16)
    wg_b, wu_b, wd_b = ffn_w_gate.astype(BF16), ffn_w_up.astype(BF16), ffn_w_down.astype(BF16)
    row3 = lambda a: a.reshape(DEPTH, 1, a.shape[-1])
    mix_g, ffn_g = row3(mix_norm_g), row3(ffn_norm_g)
    nsa_g, lru_g = row3(nsa_out_norm_g), row3(lru_out_norm_g)
    lru_cb, lam, ffn_cb = row3(lru_conv_b), row3(lru_lambda), row3(ffn_conv_b)
    final_g = final_norm_g.reshape(1, D_MODEL)

    rows16 = seq // CMP_STRIDE
    for l in range(DEPTH):
        q, kc, vc, kk, vv, zx, zy, gate = _in_call(x, mod, mix_g, w_in_p, gate_b_p, l)
        kvc = _cmp_call(kc.reshape(nb, rows16, CMP_STRIDE * KV_W), vc.reshape(nb, rows16, CMP_STRIDE * KV_W),
                        pos_p, w1_p, b1_p, w2_p, b2_p, l)
        o_nsa = _attn_call(q, kk, vv, kvc, gate, expand, overlap_t)
        o_lru = _lru_call(zx, zy, lru_conv_w, lru_cb, w_ax, bias_ax, lam, l)
        x = _out_call(x, o_nsa, o_lru, mod, nsa_g, lru_g, w_out_b, l)
        x = _ffn_call(x, mod, ffn_g, wg_b, wu_b, ffn_conv_w, ffn_cb, wd_b, final_g, l, l == DEPTH - 1)
    return x
```

```python
import functools

import numpy as np
import jax
import jax.numpy as jnp
from jax import lax
from jax.experimental import pallas as pl
from jax.experimental.pallas import tpu as pltpu

F32 = jnp.float32
BF16 = jnp.bfloat16

D_MODEL = 1024
DEPTH = 4
D_NSA = 512
D_LRU = 512
HEADS = 8
HEAD_DIM = 64
KV_GROUPS = 2
REP = HEADS // KV_GROUPS
N_BRANCH = 3
CMP_LEN = 32
CMP_STRIDE = 16
CMP_HIDDEN = 4 * HEAD_DIM
SEL_BLOCK = 64
SEL_TOP_K = 8
WINDOW = 512
Q_BLOCK = 128
LRU_BLOCKS = 8
LRU_BLOCK_W = D_LRU // LRU_BLOCKS
LRU_CONV_W = 4
LRU_C = 8.0
D_FF = 2816
FFN_CONV_W = 3
NORM_EPS = 1e-6
NEG_INF = -1e30
FORCE_BONUS = 1e4

KV_W = KV_GROUPS * HEAD_DIM
IN_COLS = D_NSA + 6 * KV_W + N_BRANCH * HEADS + 2 * D_LRU
LANES = 128
KEY_CHUNK = 256
SEL_CHUNK = 512
ROW_TILE = 512
LRU_TILE = 256
FF_CHUNK = 256
HIST = 8

C_Q = 0
C_KC = C_Q + D_NSA
C_VC = C_KC + KV_W
C_KK = C_VC + KV_W
C_VV = C_KK + KV_GROUPS * LANES
C_ZX = C_VV + KV_GROUPS * LANES
C_ZY = C_ZX + D_LRU
C_GATE = C_ZY + D_LRU
IN_COLS_P = C_GATE + KV_GROUPS * LANES


def _dot(a, b):
    return jnp.dot(a, b, preferred_element_type=F32)


def _dot_nt(a, b):
    return lax.dot_general(a, b, (((1,), (1,)), ((), ())), preferred_element_type=F32)


def _sigmoid(x):
    return 1.0 / (1.0 + jnp.exp(-x))


def _gelu_tanh(x):
    c = np.float32(np.sqrt(2.0 / np.pi))
    return x * (0.5 * (1.0 + jnp.tanh(c * (x + 0.044715 * (x * x * x)))))


def _rms(x):
    return x * lax.rsqrt(jnp.mean(x * x, axis=-1, keepdims=True) + NORM_EPS)


def _mod_kernel(c_ref, w_ref, b_ref, o_ref):
    c = c_ref[...]
    ca = (c * _sigmoid(c)).astype(BF16)
    o_ref[...] = _dot(ca, w_ref[...].astype(BF16)) + b_ref[...]


def _mod_call(c, ada_w, ada_b):
    nb = c.shape[0]
    out = pl.pallas_call(
        _mod_kernel,
        grid=(DEPTH, 6),
        in_specs=[
            pl.BlockSpec((nb, D_MODEL), lambda l, j: (0, 0)),
            pl.BlockSpec((None, D_MODEL, D_MODEL), lambda l, j: (l, 0, j)),
            pl.BlockSpec((None, 1, D_MODEL), lambda l, j: (l, 0, j)),
        ],
        out_specs=pl.BlockSpec((None, nb, D_MODEL), lambda l, j: (l, 0, j)),
        out_shape=jax.ShapeDtypeStruct((DEPTH, nb, 6 * D_MODEL), F32),
        name="adaln_mod",
    )(c, ada_w, ada_b.reshape(DEPTH, 1, 6 * D_MODEL))
    return out.reshape(DEPTH, nb, 6, D_MODEL)


def _in_kernel(x_ref, mod_ref, g_ref, w_ref, gb_ref,
               q_ref, kc_ref, vc_ref, kx_ref, kk_ref, vv_ref, zx_ref, zy_ref, gate_ref):
    x = x_ref[0]
    tm = x.shape[0]
    lane = lax.broadcasted_iota(jnp.int32, (tm, LANES), 1)
    tok = pl.program_id(1) * tm + lax.broadcasted_iota(jnp.int32, (tm, LANES), 0)
    blk_hot = jnp.where(lane - HEAD_DIM == jnp.right_shift(tok, int(np.log2(SEL_BLOCK))), 1.0, 0.0)
    h = (_rms(x) * g_ref[...]) * (1.0 + mod_ref[1:2, :]) + mod_ref[0:1, :]
    hb = h.astype(BF16)

    def seg(start, width):
        return _dot(hb, w_ref[:, start:start + width])

    q_ref[0] = (seg(C_Q, D_NSA) * (HEAD_DIM ** -0.5)).astype(BF16)
    kc_ref[0] = seg(C_KC, KV_W)
    vc_ref[0] = seg(C_VC, KV_W)
    for g in range(KV_GROUPS):
        kk = seg(C_KK + g * LANES, LANES)
        kk_ref[0, g] = kk.astype(BF16)
        kx_ref[0, g] = jnp.where(lane < HEAD_DIM, kk, blk_hot).astype(BF16)
        vv_ref[0, g] = seg(C_VV + g * LANES, LANES).astype(BF16)
        gate_ref[0, g] = _sigmoid(seg(C_GATE + g * LANES, LANES) + gb_ref[:, g * LANES:(g + 1) * LANES])
    zx_ref[0] = seg(C_ZX, D_LRU)
    zy_ref[0] = seg(C_ZY, D_LRU)


def _in_call(x, mod, norm_g, w_in_p, gate_b_p, l):
    nb, seq, _ = x.shape
    tm = ROW_TILE
    row = lambda width: pl.BlockSpec((1, tm, width), lambda b, t: (b, t, 0))
    grp = pl.BlockSpec((1, KV_GROUPS, tm, LANES), lambda b, t: (b, 0, t, 0))
    return pl.pallas_call(
        _in_kernel,
        grid=(nb, seq // tm),
        in_specs=[
            row(D_MODEL),
            pl.BlockSpec((None, None, 6, D_MODEL), lambda b, t: (l, b, 0, 0)),
            pl.BlockSpec((None, 1, D_MODEL), lambda b, t: (l, 0, 0)),
            pl.BlockSpec((None, D_MODEL, IN_COLS_P), lambda b, t: (l, 0, 0)),
            pl.BlockSpec((None, 1, KV_GROUPS * LANES), lambda b, t: (l, 0, 0)),
        ],
        out_specs=[row(D_NSA), row(KV_W), row(KV_W), grp, grp, grp, row(D_LRU), row(D_LRU), grp],
        out_shape=[
            jax.ShapeDtypeStruct((nb, seq, D_NSA), BF16),
            jax.ShapeDtypeStruct((nb, seq, KV_W), F32),
            jax.ShapeDtypeStruct((nb, seq, KV_W), F32),
            jax.ShapeDtypeStruct((nb, KV_GROUPS, seq, LANES), BF16),
            jax.ShapeDtypeStruct((nb, KV_GROUPS, seq, LANES), BF16),
            jax.ShapeDtypeStruct((nb, KV_GROUPS, seq, LANES), BF16),
            jax.ShapeDtypeStruct((nb, seq, D_LRU), F32),
            jax.ShapeDtypeStruct((nb, seq, D_LRU), F32),
            jax.ShapeDtypeStruct((nb, KV_GROUPS, seq, LANES), F32),
        ],
        compiler_params=pltpu.CompilerParams(
            dimension_semantics=("arbitrary", "arbitrary"), vmem_limit_bytes=48 * 1024 * 1024),
        name="mixer_in",
    )(x, mod, norm_g, w_in_p, gate_b_p)


def _cmp_kernel(kc_ref, vc_ref, pos_ref, w1_ref, b1_ref, w2_ref, b2_ref, o_ref):
    def hidden(x_ref, i):
        x = x_ref[0]
        top = (x + pos_ref[2 * i:2 * i + 1, :]).astype(BF16)
        bot = (x + pos_ref[2 * i + 1:2 * i + 2, :]).astype(BF16)
        a = _dot(top, w1_ref[2 * i])
        b = _dot(bot, w1_ref[2 * i + 1])
        nrow = b.shape[0]
        b = pltpu.roll(b, nrow - 1, axis=0)
        return _gelu_tanh(a + b + b1_ref[i:i + 1, :])

    hk = hidden(kc_ref, 0)
    hv = hidden(vc_ref, 1)
    for g in range(KV_GROUPS):
        sl = slice(g * CMP_HIDDEN, (g + 1) * CMP_HIDDEN)
        hg = jnp.concatenate([hk[:, sl], hv[:, sl]], axis=1).astype(BF16)
        o_ref[0, g] = (_dot(hg, w2_ref[...]) + b2_ref[...]).astype(BF16)


def _cmp_call(kc16, vc16, pos_p, w1_p, b1_p, w2_p, b2_p, l):
    nb, nrow, width = kc16.shape
    blk = pl.BlockSpec((1, nrow, width), lambda b: (b, 0, 0))
    return pl.pallas_call(
        _cmp_kernel,
        grid=(nb,),
        in_specs=[
            blk, blk,
            pl.BlockSpec((None, 4, width), lambda b: (l, 0, 0)),
            pl.BlockSpec((None, 4, width, KV_GROUPS * CMP_HIDDEN), lambda b: (l, 0, 0, 0)),
            pl.BlockSpec((None, 2, KV_GROUPS * CMP_HIDDEN), lambda b: (l, 0, 0)),
            pl.BlockSpec((None, 2 * CMP_HIDDEN, LANES), lambda b: (l, 0, 0)),
            pl.BlockSpec((None, 1, LANES), lambda b: (l, 0, 0)),
        ],
        out_specs=pl.BlockSpec((1, KV_GROUPS, nrow, LANES), lambda b: (b, 0, 0, 0)),
        out_shape=jax.ShapeDtypeStruct((nb, KV_GROUPS, nrow, LANES), BF16),
        compiler_params=pltpu.CompilerParams(
            dimension_semantics=("arbitrary",), vmem_limit_bytes=48 * 1024 * 1024),
        name="nsa_compress",
    )(kc16, vc16, pos_p, w1_p, b1_p, w2_p, b2_p)


def _attn_kernel(q_ref, kx_ref, kk_ref, vv_ref, kvc_ref, gate_ref, ot_ref, eg_ref, o_ref,
                 s_ref, acc_ref):
    qb = pl.program_id(2)
    nq = Q_BLOCK
    nrow = REP * nq
    q = q_ref[0]
    qs = jnp.concatenate([q[:, r * HEAD_DIM:(r + 1) * HEAD_DIM] for r in range(REP)], axis=0)
    zpad = jnp.zeros_like(qs)
    q_lo = jnp.concatenate([qs, zpad], axis=1)
    q_hi = jnp.concatenate([zpad, qs], axis=1)

    def tok_rows(width):
        return qb * nq + lax.broadcasted_iota(jnp.int32, (nq, width), 0)

    lane_row = lax.broadcasted_iota(jnp.int32, (nrow, LANES), 1)

    def lane_max(x):
        return functools.reduce(jnp.maximum, [x[:, i:i + LANES] for i in range(0, x.shape[1], LANES)])

    def lane_sum(x):
        return functools.reduce(jnp.add, [x[:, i:i + LANES] for i in range(0, x.shape[1], LANES)])

    def rep_lanes(x, width):
        return jnp.concatenate([x] * (width // LANES), axis=1)

    def row_bcast(x):
        return jnp.broadcast_to(x, (x.shape[0], LANES))

    gt = gate_ref[0, 0]
    g_hi = gt.astype(BF16)
    g_lo = (gt - g_hi.astype(F32)).astype(BF16)
    gexp = _dot(g_hi, eg_ref[...]) + _dot(g_lo, eg_ref[...])

    def gate_rows(br):
        return jnp.concatenate(
            [gexp[:, (N_BRANCH * r + br) * LANES:(N_BRANCH * r + br + 1) * LANES] for r in range(REP)], axis=0)

    kvc = kvc_ref[0, 0]
    ncb = kvc.shape[0]
    sc = _dot_nt(q_lo, kvc).reshape(REP, nq, ncb)
    cidx = lax.broadcasted_iota(jnp.int32, (nq, ncb), 1)
    mask_c = (cidx * CMP_STRIDE + (CMP_LEN - 1)) <= tok_rows(ncb)
    sc = jnp.where(mask_c[None], sc, NEG_INF)
    mc = jnp.max(sc, axis=-1, keepdims=True)
    pc = jnp.where(mask_c[None], jnp.exp(sc - mc), 0.0)
    lc = jnp.sum(pc, axis=-1, keepdims=True)
    pc = pc / jnp.where(lc > 0.0, lc, 1.0)
    o_cmp = _dot(pc.reshape(nrow, ncb).astype(BF16), kvc)

    psum = pc[0] + pc[1] + pc[2] + pc[3]
    p_hi = psum.astype(BF16)
    p_lo = (psum - p_hi.astype(F32)).astype(BF16)
    ot = ot_ref[...]
    imp = _dot_nt(ot, p_hi) + _dot_nt(ot, p_lo)
    nblk = imp.shape[0]
    jj = lax.broadcasted_iota(jnp.int32, (nblk, nq), 0)
    lane = lax.broadcasted_iota(jnp.int32, (nblk, nq), 1)
    qblk = jnp.right_shift(qb * nq + lane, int(np.log2(SEL_BLOCK)))
    valid = jj <= qblk
    forced = (jj == 0) | (jj == qblk) | (jj == qblk - 1)
    imp = jnp.where(valid, imp + jnp.where(forced, FORCE_BONUS, 0.0), NEG_INF)
    rank = jnp.zeros((nblk, nq), F32)
    for i in range(nblk):
        row = imp[i:i + 1, :]
        beats = (row > imp) | ((row == imp) & (jj > i))
        rank = rank + jnp.where(beats, 1.0, 0.0)
    sel_t = jnp.where(rank < float(SEL_TOP_K), 1.0, 0.0)
    sel_t = jnp.concatenate(
        [sel_t, jnp.ones((HEAD_DIM - nblk, nq), F32), jnp.zeros((LANES - HEAD_DIM, nq), F32)], axis=0)
    selb = ((sel_t.T[:, :HEAD_DIM] - 1.0) * -NEG_INF).astype(BF16)
    q_sel = jnp.concatenate([qs, jnp.concatenate([selb] * REP, axis=0)], axis=1)

    n_win = WINDOW // KEY_CHUNK + 1
    diag = qb // (KEY_CHUNK // nq)
    s_win, v_win = [], []
    for i in range(n_win):
        cw = diag - (n_win - 1) + i
        k0 = pl.multiple_of(jnp.maximum(cw, 0) * KEY_CHUNK, KEY_CHUNK)
        kpos = cw * KEY_CHUNK + lax.broadcasted_iota(jnp.int32, (nq, KEY_CHUNK), 1)
        tq = tok_rows(KEY_CHUNK)
        ok = (kpos <= tq) & (kpos > tq - WINDOW) & (kpos >= 0)
        s = _dot_nt(q_hi, kk_ref[0, 0, pl.ds(k0, KEY_CHUNK), :]).reshape(REP, nq, KEY_CHUNK)
        s_win.append(jnp.where(ok[None], s, NEG_INF).reshape(nrow, KEY_CHUNK))
        v_win.append(vv_ref[0, 0, pl.ds(k0, KEY_CHUNK), :])
    m_w = row_bcast(jnp.max(functools.reduce(jnp.maximum, [lane_max(s) for s in s_win]), axis=1, keepdims=True))
    p_win = [jnp.exp(s - rep_lanes(m_w, KEY_CHUNK)) for s in s_win]
    l_w = row_bcast(jnp.sum(functools.reduce(jnp.add, [lane_sum(p) for p in p_win]), axis=1, keepdims=True))
    acc_w = functools.reduce(jnp.add, [_dot(p.astype(BF16), v) for p, v in zip(p_win, v_win)])
    o_hi = (gate_rows(2) / l_w) * acc_w + gate_rows(0) * o_cmp

    n_sel = qb // (SEL_CHUNK // nq) + 1

    def sel_scores(c, causal):
        k0 = pl.multiple_of(c * SEL_CHUNK, SEL_CHUNK)
        s = _dot_nt(q_sel, kx_ref[0, 0, pl.ds(k0, SEL_CHUNK), :])
        if causal:
            kpos = k0 + lax.broadcasted_iota(jnp.int32, (nq, SEL_CHUNK), 1)
            ok = kpos <= tok_rows(SEL_CHUNK)
            s = jnp.where(ok[None], s.reshape(REP, nq, SEL_CHUNK), NEG_INF).reshape(nrow, SEL_CHUNK)
        s_ref[:, pl.ds(k0, SEL_CHUNK)] = s
        return lane_max(s)

    def pass1(c, m_run):
        return jnp.maximum(m_run, sel_scores(c, False))

    m_run = lax.fori_loop(0, n_sel - 1, pass1, jnp.full((nrow, LANES), NEG_INF, F32))
    m_run = jnp.maximum(m_run, sel_scores(n_sel - 1, True))
    m_s = row_bcast(jnp.max(m_run, axis=1, keepdims=True))
    acc_ref[...] = jnp.zeros(acc_ref.shape, F32)

    def pass2(c, l_run):
        k0 = pl.multiple_of(c * SEL_CHUNK, SEL_CHUNK)
        p = jnp.exp(s_ref[:, pl.ds(k0, SEL_CHUNK)] - rep_lanes(m_s, SEL_CHUNK))
        acc_ref[...] += _dot(p.astype(BF16), vv_ref[0, 0, pl.ds(k0, SEL_CHUNK), :])
        return l_run + lane_sum(p)

    l_run = lax.fori_loop(0, n_sel, pass2, jnp.zeros((nrow, LANES), F32))
    l_s = row_bcast(jnp.sum(l_run, axis=1, keepdims=True))
    o_lo = (gate_rows(1) / l_s) * acc_ref[...]

    t = jnp.where(lane_row < HEAD_DIM, o_lo, o_hi)
    u = t + pltpu.roll(t, HEAD_DIM, axis=1)
    lane_q = lax.broadcasted_iota(jnp.int32, (nq, LANES), 1)
    o_ref[0] = jnp.concatenate(
        [jnp.where(lane_q < HEAD_DIM, u[(2 * i) * nq:(2 * i + 1) * nq], u[(2 * i + 1) * nq:(2 * i + 2) * nq])
         for i in range(REP // 2)], axis=1)


def _attn_call(q, kx, kk, vv, kvc, gate, overlap_t, gate_expand):
    nb, seq, _ = q.shape
    nq = seq // Q_BLOCK
    ncb = kvc.shape[2]
    per_bg = lambda rows: pl.BlockSpec((1, 1, rows, LANES), lambda b, g, i: (b, g, 0, 0))
    return pl.pallas_call(
        _attn_kernel,
        grid=(nb, KV_GROUPS, nq),
        in_specs=[
            pl.BlockSpec((1, Q_BLOCK, REP * HEAD_DIM), lambda b, g, i: (b, i, g)),
            per_bg(seq), per_bg(seq), per_bg(seq), per_bg(ncb),
            pl.BlockSpec((1, 1, Q_BLOCK, LANES), lambda b, g, i: (b, g, i, 0)),
            pl.BlockSpec(overlap_t.shape, lambda b, g, i: (0, 0)),
            pl.BlockSpec(gate_expand.shape, lambda b, g, i: (0, 0)),
        ],
        out_specs=pl.BlockSpec((1, Q_BLOCK, REP * HEAD_DIM), lambda b, g, i: (b, i, g)),
        out_shape=jax.ShapeDtypeStruct((nb, seq, D_NSA), F32),
        scratch_shapes=[
            pltpu.VMEM((REP * Q_BLOCK, seq), F32),
            pltpu.VMEM((REP * Q_BLOCK, LANES), F32),
        ],
        compiler_params=pltpu.CompilerParams(
            dimension_semantics=("arbitrary", "arbitrary", "arbitrary"),
            vmem_limit_bytes=48 * 1024 * 1024),
        name="nsa_attention",
    )(q, kx, kk, vv, kvc, gate, overlap_t, gate_expand)


def _lru_kernel(zx_ref, zy_ref, cw_ref, cb_ref, w_ref, bias_ref, lam_ref, o_ref, hist_ref, h_ref):
    t = pl.program_id(1)
    tl = zx_ref.shape[1]

    @pl.when(t == 0)
    def _():
        hist_ref[...] = jnp.zeros(hist_ref.shape, F32)
        h_ref[...] = jnp.zeros(h_ref.shape, F32)

    x = zx_ref[0]
    xe = jnp.concatenate([hist_ref[...], x], axis=0)
    hist_ref[...] = x[tl - HIST:, :]
    xc = cb_ref[...] + xe[HIST:, :] * cw_ref[LRU_CONV_W - 1:LRU_CONV_W, :]
    for k in range(LRU_CONV_W - 1):
        off = HIST - (LRU_CONV_W - 1) + k
        xc = xc + xe[off:off + tl, :] * cw_ref[k:k + 1, :]

    y = _dot(xc.astype(BF16), w_ref[...])
    r = _sigmoid(y[:, :D_LRU] + bias_ref[0:1, :])
    i = _sigmoid(y[:, D_LRU:] + bias_ref[1:2, :])
    lam = lam_ref[...]
    log_sig = -(jnp.maximum(-lam, 0.0) + jnp.log1p(jnp.exp(-jnp.abs(lam))))
    log_a = LRU_C * r * log_sig
    a = jnp.exp(log_a)
    u = jnp.sqrt(-jnp.tanh(log_a) * (a * a + 1.0)) * (i * xc)

    rows = lax.broadcasted_iota(jnp.int32, (tl, D_LRU), 0)
    d = 1
    while d < tl:
        ok = rows >= d
        a_sh = pltpu.roll(a, d, axis=0)
        u_sh = pltpu.roll(u, d, axis=0)
        u = jnp.where(ok, a * u_sh + u, u)
        a = jnp.where(ok, a * a_sh, a)
        d *= 2
    h = a * h_ref[HIST - 1:HIST, :] + u
    h_ref[...] = h[tl - HIST:, :]
    o_ref[0] = h * _gelu_tanh(zy_ref[0])


def _lru_call(zx, zy, conv_w, conv_b, w_ax, bias_ax, lam, l):
    nb, seq, _ = zx.shape
    tl = LRU_TILE
    row = pl.BlockSpec((1, tl, D_LRU), lambda b, t: (b, t, 0))
    return pl.pallas_call(
        _lru_kernel,
        grid=(nb, seq // tl),
        in_specs=[
            row, row,
            pl.BlockSpec((None, LRU_CONV_W, D_LRU), lambda b, t: (l, 0, 0)),
            pl.BlockSpec((None, 1, D_LRU), lambda b, t: (l, 0, 0)),
            pl.BlockSpec((None, D_LRU, 2 * D_LRU), lambda b, t: (l, 0, 0)),
            pl.BlockSpec((None, 2, D_LRU), lambda b, t: (l, 0, 0)),
            pl.BlockSpec((None, 1, D_LRU), lambda b, t: (l, 0, 0)),
        ],
        out_specs=row,
        out_shape=jax.ShapeDtypeStruct((nb, seq, D_LRU), F32),
        scratch_shapes=[pltpu.VMEM((HIST, D_LRU), F32), pltpu.VMEM((HIST, D_LRU), F32)],
        compiler_params=pltpu.CompilerParams(dimension_semantics=("arbitrary", "arbitrary")),
        name="rglru",
    )(zx, zy, conv_w, conv_b, w_ax, bias_ax, lam)


def _out_kernel(x_ref, on_ref, ol_ref, mod_ref, gn_ref, gl_ref, w_ref, o_ref):
    a = (_rms(on_ref[0]) * gn_ref[...]).astype(BF16)
    b = (_rms(ol_ref[0]) * gl_ref[...]).astype(BF16)
    y = _dot(a, w_ref[:D_NSA, :]) + _dot(b, w_ref[D_NSA:, :])
    o_ref[0] = x_ref[0] + mod_ref[2:3, :] * y


def _out_call(x, o_nsa, o_lru, mod, g_nsa, g_lru, w_out, l):
    nb, seq, _ = x.shape
    tm = ROW_TILE
    row = lambda width: pl.BlockSpec((1, tm, width), lambda b, t: (b, t, 0))
    return pl.pallas_call(
        _out_kernel,
        grid=(nb, seq // tm),
        in_specs=[
            row(D_MODEL), row(D_NSA), row(D_LRU),
            pl.BlockSpec((None, None, 6, D_MODEL), lambda b, t: (l, b, 0, 0)),
            pl.BlockSpec((None, 1, D_NSA), lambda b, t: (l, 0, 0)),
            pl.BlockSpec((None, 1, D_LRU), lambda b, t: (l, 0, 0)),
            pl.BlockSpec((None, D_MODEL, D_MODEL), lambda b, t: (l, 0, 0)),
        ],
        out_specs=row(D_MODEL),
        out_shape=jax.ShapeDtypeStruct(x.shape, F32),
        compiler_params=pltpu.CompilerParams(
            dimension_semantics=("arbitrary", "arbitrary"), vmem_limit_bytes=48 * 1024 * 1024),
        name="mixer_out",
    )(x, o_nsa, o_lru, mod, g_nsa, g_lru, w_out)


def _ffn_kernel(x_ref, mod_ref, g_ref, wg_ref, wu_ref, cw_ref, cb_ref, wd_ref, fg_ref, o_ref,
                hist_ref, acc_ref, *, final_norm):
    t = pl.program_id(1)
    tm = x_ref.shape[1]

    @pl.when(t == 0)
    def _():
        hist_ref[...] = jnp.zeros(hist_ref.shape, F32)

    x = x_ref[0]
    h = (_rms(x) * g_ref[...]) * (1.0 + mod_ref[4:5, :]) + mod_ref[3:4, :]
    hb = h.astype(BF16)
    for j in range(D_FF // FF_CHUNK):
        cs = slice(j * FF_CHUNK, (j + 1) * FF_CHUNK)
        gp = _dot(hb, wg_ref[:, cs])
        ge = jnp.concatenate([hist_ref[:, cs], gp], axis=0)
        hist_ref[:, cs] = gp[tm - HIST:, :]
        gate = cb_ref[:, cs] + gp * cw_ref[FFN_CONV_W - 1:FFN_CONV_W, cs]
        for k in range(FFN_CONV_W - 1):
            off = HIST - (FFN_CONV_W - 1) + k
            gate = gate + ge[off:off + tm, :] * cw_ref[k:k + 1, cs]
        act = (gate * _sigmoid(gate)) * _dot(hb, wu_ref[:, cs])
        contrib = _dot(act.astype(BF16), wd_ref[cs, :])
        if j == 0:
            acc_ref[...] = contrib
        else:
            acc_ref[...] += contrib
    y = x + mod_ref[5:6, :] * acc_ref[...]
    if final_norm:
        y = _rms(y) * fg_ref[...]
    o_ref[0] = y


def _ffn_call(x, mod, norm_g, w_gate, w_up, conv_w, conv_b, w_down, final_g, l, final_norm):
    nb, seq, _ = x.shape
    tm = ROW_TILE
    row = pl.BlockSpec((1, tm, D_MODEL), lambda b, t: (b, t, 0))
    once = dict(pipeline_mode=pl.Buffered(1))
    return pl.pallas_call(
        functools.partial(_ffn_kernel, final_norm=final_norm),
        grid=(nb, seq // tm),
        in_specs=[
            row,
            pl.BlockSpec((None, None, 6, D_MODEL), lambda b, t: (l, b, 0, 0)),
            pl.BlockSpec((None, 1, D_MODEL), lambda b, t: (l, 0, 0)),
            pl.BlockSpec((None, D_MODEL, D_FF), lambda b, t: (l, 0, 0), **once),
            pl.BlockSpec((None, D_MODEL, D_FF), lambda b, t: (l, 0, 0), **once),
            pl.BlockSpec((None, FFN_CONV_W, D_FF), lambda b, t: (l, 0, 0)),
            pl.BlockSpec((None, 1, D_FF), lambda b, t: (l, 0, 0)),
            pl.BlockSpec((None, D_FF, D_MODEL), lambda b, t: (l, 0, 0), **once),
            pl.BlockSpec((1, D_MODEL), lambda b, t: (0, 0)),
        ],
        out_specs=row,
        out_shape=jax.ShapeDtypeStruct(x.shape, F32),
        scratch_shapes=[pltpu.VMEM((HIST, D_FF), F32), pltpu.VMEM((tm, D_MODEL), F32)],
        compiler_params=pltpu.CompilerParams(
            dimension_semantics=("arbitrary", "arbitrary"), vmem_limit_bytes=56 * 1024 * 1024),
        name="conv_ffn",
    )(x, mod, norm_g, w_gate, w_up, conv_w, conv_b, w_down, final_g)


def _in_perm():
    zero = IN_COLS
    q0, kc0, vc0 = 0, D_NSA, D_NSA + KV_W
    ks0, vs0, kw0, vw0 = (D_NSA + i * KV_W for i in range(2, 6))
    zg0 = D_NSA + 6 * KV_W
    zx0 = zg0 + N_BRANCH * HEADS
    zy0 = zx0 + D_LRU
    cols = list(range(q0, q0 + D_NSA)) + list(range(kc0, kc0 + KV_W)) + list(range(vc0, vc0 + KV_W))
    for a0, b0 in ((ks0, kw0), (vs0, vw0)):
        for g in range(KV_GROUPS):
            cols += list(range(a0 + g * HEAD_DIM, a0 + (g + 1) * HEAD_DIM))
            cols += list(range(b0 + g * HEAD_DIM, b0 + (g + 1) * HEAD_DIM))
    cols += list(range(zx0, zx0 + D_LRU)) + list(range(zy0, zy0 + D_LRU))
    per_g = N_BRANCH * REP
    for g in range(KV_GROUPS):
        cols += list(range(zg0 + g * per_g, zg0 + (g + 1) * per_g)) + [zero] * (LANES - per_g)
    assert len(cols) == IN_COLS_P
    return np.asarray(cols, np.int32)


def _prep_in(w_in, gate_b):
    w_aug = jnp.concatenate([w_in, jnp.zeros(w_in.shape[:-1] + (1,), w_in.dtype)], axis=-1)
    w_p = jnp.take(w_aug, _in_perm(), axis=-1).astype(BF16)
    per_g = N_BRANCH * REP
    gb = gate_b.reshape(DEPTH, KV_GROUPS, per_g)
    gb = jnp.pad(gb, ((0, 0), (0, 0), (0, LANES - per_g))).reshape(DEPTH, 1, KV_GROUPS * LANES)
    return w_p, gb


def _prep_cmp(cmp_pos, cmp_w1, cmp_b1, cmp_w2, cmp_b2):
    half = CMP_LEN // 2
    pos = cmp_pos.reshape(DEPTH, 2, 2, half, 1, HEAD_DIM)
    pos = jnp.broadcast_to(pos, (DEPTH, 2, 2, half, KV_GROUPS, HEAD_DIM))
    pos_p = pos.reshape(DEPTH, 4, half * KV_W)
    w1 = cmp_w1.reshape(DEPTH, 2, 2, half, HEAD_DIM, CMP_HIDDEN)
    eye = jnp.eye(KV_GROUPS, dtype=w1.dtype)
    w1_p = jnp.einsum('lvhtdc,gG->lvhtgdGc', w1, eye)
    w1_p = w1_p.reshape(DEPTH, 4, half * KV_W, KV_GROUPS * CMP_HIDDEN).astype(BF16)
    b1_p = jnp.tile(cmp_b1, (1, 1, KV_GROUPS))
    zeros = jnp.zeros((DEPTH, CMP_HIDDEN, HEAD_DIM), cmp_w2.dtype)
    w2_p = jnp.concatenate([
        jnp.concatenate([cmp_w2[:, 0], zeros], axis=-1),
        jnp.concatenate([zeros, cmp_w2[:, 1]], axis=-1)], axis=1).astype(BF16)
    b2_p = cmp_b2.reshape(DEPTH, 1, 2 * HEAD_DIM)
    return pos_p, w1_p, b1_p, w2_p, b2_p


def _prep_lru(lru_wa, lru_wx, lru_ba, lru_bx):
    eye = jnp.eye(LRU_BLOCKS, dtype=lru_wa.dtype)
    dense = lambda w: jnp.einsum('lncd,nm->lncmd', w, eye).reshape(DEPTH, D_LRU, D_LRU)
    w_ax = jnp.concatenate([dense(lru_wa), dense(lru_wx)], axis=-1).astype(BF16)
    bias_ax = jnp.stack([lru_ba, lru_bx], axis=1)
    return w_ax, bias_ax


def _attn_consts(seq):
    nblk = seq // SEL_BLOCK
    ncb_p = seq // CMP_STRIDE
    cstart = np.arange(ncb_p) * CMP_STRIDE
    js = np.arange(nblk) * SEL_BLOCK
    overlap_t = ((cstart[None, :] < js[:, None] + SEL_BLOCK) & (cstart[None, :] + CMP_LEN > js[:, None]))
    n_gate = N_BRANCH * REP
    gate_expand = np.zeros((LANES, n_gate * LANES), np.float32)
    for i in range(n_gate):
        gate_expand[i, i * LANES:(i + 1) * LANES] = 1.0
    return jnp.asarray(overlap_t.astype(np.float32), BF16), jnp.asarray(gate_expand, BF16)


def kernel(x, c, ada_w, ada_b, mix_norm_g, ffn_norm_g, w_in, nsa_gate_b, cmp_pos, cmp_w1, cmp_b1, cmp_w2, cmp_b2, lru_conv_w, lru_conv_b, lru_wa, lru_ba, lru_wx, lru_bx, lru_lambda, nsa_out_norm_g, lru_out_norm_g, w_out, ffn_w_gate, ffn_w_up, ffn_conv_w, ffn_conv_b, ffn_w_down, final_norm_g):
    nb, seq, _ = x.shape
    mod = _mod_call(c, ada_w, ada_b)
    w_in_p, gate_b_p = _prep_in(w_in, nsa_gate_b)
    pos_p, w1_p, b1_p, w2_p, b2_p = _prep_cmp(cmp_pos, cmp_w1, cmp_b1, cmp_w2, cmp_b2)
    w_ax, bias_ax = _prep_lru(lru_wa, lru_wx, lru_ba, lru_bx)
    overlap_t, gate_expand = _attn_consts(seq)
    w_out_b = w_out.astype(BF16)
    wg_b, wu_b, wd_b = ffn_w_gate.astype(BF16), ffn_w_up.astype(BF16), ffn_w_down.astype(BF16)
    row3 = lambda a: a.reshape(DEPTH, 1, a.shape[-1])
    mix_g, ffn_g = row3(mix_norm_g), row3(ffn_norm_g)
    nsa_g, lru_g = row3(nsa_out_norm_g), row3(lru_out_norm_g)
    lru_cb, lam, ffn_cb = row3(lru_conv_b), row3(lru_lambda), row3(ffn_conv_b)
    final_g = final_norm_g.reshape(1, D_MODEL)

    rows16 = seq // CMP_STRIDE
    for l in range(DEPTH):
        q, kc, vc, kx, kk, vv, zx, zy, gate = _in_call(x, mod, mix_g, w_in_p, gate_b_p, l)
        kvc = _cmp_call(kc.reshape(nb, rows16, CMP_STRIDE * KV_W), vc.reshape(nb, rows16, CMP_STRIDE * KV_W),
                        pos_p, w1_p, b1_p, w2_p, b2_p, l)
        o_nsa = _attn_call(q, kx, kk, vv, kvc, gate, overlap_t, gate_expand)
        o_lru = _lru_call(zx, zy, lru_conv_w, lru_cb, w_ax, bias_ax, lam, l)
        x = _out_call(x, o_nsa, o_lru, mod, nsa_g, lru_g, w_out_b, l)
        x = _ffn_call(x, mod, ffn_g, wg_b, wu_b, ffn_conv_w, ffn_cb, wd_b, final_g, l, l == DEPTH - 1)
    return x
```

```python
import functools

import numpy as np
import jax
import jax.numpy as jnp
from jax import lax
from jax.experimental import pallas as pl
from jax.experimental.pallas import tpu as pltpu

F32 = jnp.float32
BF16 = jnp.bfloat16

D_MODEL = 1024
DEPTH = 4
D_NSA = 512
D_LRU = 512
HEADS = 8
HEAD_DIM = 64
KV_GROUPS = 2
REP = HEADS // KV_GROUPS
N_BRANCH = 3
CMP_LEN = 32
CMP_STRIDE = 16
CMP_HIDDEN = 4 * HEAD_DIM
SEL_BLOCK = 64
SEL_TOP_K = 8
WINDOW = 512
Q_BLOCK = 128
LRU_BLOCKS = 8
LRU_BLOCK_W = D_LRU // LRU_BLOCKS
LRU_CONV_W = 4
LRU_C = 8.0
D_FF = 2816
FFN_CONV_W = 3
NORM_EPS = 1e-6
NEG_INF = -1e30
FORCE_BONUS = 1e4

KV_W = KV_GROUPS * HEAD_DIM
IN_COLS = D_NSA + 6 * KV_W + N_BRANCH * HEADS + 2 * D_LRU
LANES = 128
KEY_CHUNK = 256
SEL_CHUNK = 512
ROW_TILE = 512
LRU_TILE = 256
FF_CHUNK = 256
HIST = 8

C_Q = 0
C_KC = C_Q + D_NSA
C_VC = C_KC + KV_W
C_KK = C_VC + KV_W
C_VV = C_KK + KV_GROUPS * LANES
C_ZX = C_VV + KV_GROUPS * LANES
C_ZY = C_ZX + D_LRU
C_GATE = C_ZY + D_LRU
IN_COLS_P = C_GATE + KV_GROUPS * LANES


def _dot(a, b):
    return jnp.dot(a, b, preferred_element_type=F32)


def _dot_nt(a, b):
    return lax.dot_general(a, b, (((1,), (1,)), ((), ())), preferred_element_type=F32)


def _sigmoid(x):
    return 1.0 / (1.0 + jnp.exp(-x))


def _gelu_tanh(x):
    c = np.float32(np.sqrt(2.0 / np.pi))
    return x * (0.5 * (1.0 + jnp.tanh(c * (x + 0.044715 * (x * x * x)))))


def _rms(x):
    return x * lax.rsqrt(jnp.mean(x * x, axis=-1, keepdims=True) + NORM_EPS)


def _mod_kernel(c_ref, w_ref, b_ref, o_ref):
    c = c_ref[...]
    ca = (c * _sigmoid(c)).astype(BF16)
    o_ref[...] = _dot(ca, w_ref[...].astype(BF16)) + b_ref[...]


def _mod_call(c, ada_w, ada_b):
    nb = c.shape[0]
    out = pl.pallas_call(
        _mod_kernel,
        grid=(DEPTH, 6),
        in_specs=[
            pl.BlockSpec((nb, D_MODEL), lambda l, j: (0, 0)),
            pl.BlockSpec((None, D_MODEL, D_MODEL), lambda l, j: (l, 0, j)),
            pl.BlockSpec((None, 1, D_MODEL), lambda l, j: (l, 0, j)),
        ],
        out_specs=pl.BlockSpec((None, nb, D_MODEL), lambda l, j: (l, 0, j)),
        out_shape=jax.ShapeDtypeStruct((DEPTH, nb, 6 * D_MODEL), F32),
        name="adaln_mod",
    )(c, ada_w, ada_b.reshape(DEPTH, 1, 6 * D_MODEL))
    return out.reshape(DEPTH, nb, 6, D_MODEL)


def _in_kernel(x_ref, mod_ref, g_ref, w_ref, gb_ref,
               q_ref, kc_ref, vc_ref, kx_ref, kk_ref, vs_ref, vw_ref, zx_ref, zy_ref, gate_ref, tok_ref):
    x = x_ref[0]
    tm = x.shape[0]
    lane = lax.broadcasted_iota(jnp.int32, (tm, LANES), 1)
    tok = pl.program_id(1) * tm + lax.broadcasted_iota(jnp.int32, (tm, LANES), 0)
    blk_hot = jnp.where(lane - HEAD_DIM == jnp.right_shift(tok, int(np.log2(SEL_BLOCK))), 1.0, 0.0)
    h = (_rms(x) * g_ref[...]) * (1.0 + mod_ref[1:2, :]) + mod_ref[0:1, :]
    hb = h.astype(BF16)

    def seg(start, width):
        return _dot(hb, w_ref[:, start:start + width])

    q_ref[0] = (seg(C_Q, D_NSA) * (HEAD_DIM ** -0.5)).astype(BF16)
    for out_ref, start in ((kc_ref, C_KC), (vc_ref, C_VC)):
        tok_ref[...] = seg(start, KV_W)
        for i in range(CMP_STRIDE):
            out_ref[0, :, i * KV_W:(i + 1) * KV_W] = tok_ref[pl.ds(i, tm // CMP_STRIDE, stride=CMP_STRIDE), :]
    for g in range(KV_GROUPS):
        kk = seg(C_KK + g * LANES, LANES)
        kk_ref[0, g] = kk.astype(BF16)
        kx_ref[0, g] = jnp.where(lane < HEAD_DIM, kk, blk_hot).astype(BF16)
        vv = seg(C_VV + g * LANES, LANES)
        vs_ref[0, g] = jnp.where(lane < HEAD_DIM, vv, 1.0).astype(BF16)
        vw_ref[0, g] = jnp.where(lane < HEAD_DIM, 1.0, vv).astype(BF16)
        gate_ref[0, g] = _sigmoid(seg(C_GATE + g * LANES, LANES) + gb_ref[:, g * LANES:(g + 1) * LANES])
    zx_ref[0] = seg(C_ZX, D_LRU)
    zy_ref[0] = seg(C_ZY, D_LRU)


def _in_call(x, mod, norm_g, w_in_p, gate_b_p, l):
    nb, seq, _ = x.shape
    tm = ROW_TILE
    row = lambda width: pl.BlockSpec((1, tm, width), lambda b, t: (b, t, 0))
    grp = pl.BlockSpec((1, KV_GROUPS, tm, LANES), lambda b, t: (b, 0, t, 0))
    cmp_rows, cmp_width = tm // CMP_STRIDE, CMP_STRIDE * KV_W
    cmp_in = pl.BlockSpec((1, cmp_rows, cmp_width), lambda b, t: (b, t, 0))
    return pl.pallas_call(
        _in_kernel,
        grid=(nb, seq // tm),
        in_specs=[
            row(D_MODEL),
            pl.BlockSpec((None, None, 6, D_MODEL), lambda b, t: (l, b, 0, 0)),
            pl.BlockSpec((None, 1, D_MODEL), lambda b, t: (l, 0, 0)),
            pl.BlockSpec((None, D_MODEL, IN_COLS_P), lambda b, t: (l, 0, 0)),
            pl.BlockSpec((None, 1, KV_GROUPS * LANES), lambda b, t: (l, 0, 0)),
        ],
        out_specs=[row(D_NSA), cmp_in, cmp_in, grp, grp, grp, grp, row(D_LRU), row(D_LRU), grp],
        out_shape=[
            jax.ShapeDtypeStruct((nb, seq, D_NSA), BF16),
            jax.ShapeDtypeStruct((nb, seq // CMP_STRIDE, cmp_width), F32),
            jax.ShapeDtypeStruct((nb, seq // CMP_STRIDE, cmp_width), F32),
            jax.ShapeDtypeStruct((nb, KV_GROUPS, seq, LANES), BF16),
            jax.ShapeDtypeStruct((nb, KV_GROUPS, seq, LANES), BF16),
            jax.ShapeDtypeStruct((nb, KV_GROUPS, seq, LANES), BF16),
            jax.ShapeDtypeStruct((nb, KV_GROUPS, seq, LANES), BF16),
            jax.ShapeDtypeStruct((nb, seq, D_LRU), F32),
            jax.ShapeDtypeStruct((nb, seq, D_LRU), F32),
            jax.ShapeDtypeStruct((nb, KV_GROUPS, seq, LANES), F32),
        ],
        scratch_shapes=[pltpu.VMEM((tm, KV_W), F32)],
        compiler_params=pltpu.CompilerParams(
            dimension_semantics=("arbitrary", "arbitrary"), vmem_limit_bytes=48 * 1024 * 1024),
        name="mixer_in",
    )(x, mod, norm_g, w_in_p, gate_b_p)


def _cmp_kernel(kc_ref, vc_ref, pos_ref, w1_ref, b1_ref, w2_ref, b2_ref, o_ref):
    def hidden(x_ref, i):
        x = x_ref[0]
        top = (x + pos_ref[2 * i:2 * i + 1, :]).astype(BF16)
        bot = (x + pos_ref[2 * i + 1:2 * i + 2, :]).astype(BF16)
        a = _dot(top, w1_ref[2 * i])
        b = _dot(bot, w1_ref[2 * i + 1])
        nrow = b.shape[0]
        b = pltpu.roll(b, nrow - 1, axis=0)
        return _gelu_tanh(a + b + b1_ref[i:i + 1, :])

    hk = hidden(kc_ref, 0)
    hv = hidden(vc_ref, 1)
    for g in range(KV_GROUPS):
        sl = slice(g * CMP_HIDDEN, (g + 1) * CMP_HIDDEN)
        hg = jnp.concatenate([hk[:, sl], hv[:, sl]], axis=1).astype(BF16)
        o_ref[0, g] = (_dot(hg, w2_ref[...]) + b2_ref[...]).astype(BF16)


def _cmp_call(kc16, vc16, pos_p, w1_p, b1_p, w2_p, b2_p, l):
    nb, nrow, width = kc16.shape
    blk = pl.BlockSpec((1, nrow, width), lambda b: (b, 0, 0))
    return pl.pallas_call(
        _cmp_kernel,
        grid=(nb,),
        in_specs=[
            blk, blk,
            pl.BlockSpec((None, 4, width), lambda b: (l, 0, 0)),
            pl.BlockSpec((None, 4, width, KV_GROUPS * CMP_HIDDEN), lambda b: (l, 0, 0, 0)),
            pl.BlockSpec((None, 2, KV_GROUPS * CMP_HIDDEN), lambda b: (l, 0, 0)),
            pl.BlockSpec((None, 2 * CMP_HIDDEN, LANES), lambda b: (l, 0, 0)),
            pl.BlockSpec((None, 1, LANES), lambda b: (l, 0, 0)),
        ],
        out_specs=pl.BlockSpec((1, KV_GROUPS, nrow, LANES), lambda b: (b, 0, 0, 0)),
        out_shape=jax.ShapeDtypeStruct((nb, KV_GROUPS, nrow, LANES), BF16),
        compiler_params=pltpu.CompilerParams(
            dimension_semantics=("arbitrary",), vmem_limit_bytes=48 * 1024 * 1024),
        name="nsa_compress",
    )(kc16, vc16, pos_p, w1_p, b1_p, w2_p, b2_p)


def _attn_kernel(q_ref, kx_ref, kk_ref, vs_ref, vw_ref, kvc_ref, gate_ref, ot_ref, eg_ref, o_ref,
                 s_ref, acc_ref):
    qb = pl.program_id(2)
    nq = Q_BLOCK
    nrow = REP * nq
    q = q_ref[0]
    qs = jnp.concatenate([q[:, r * HEAD_DIM:(r + 1) * HEAD_DIM] for r in range(REP)], axis=0)
    zpad = jnp.zeros_like(qs)
    q_lo = jnp.concatenate([qs, zpad], axis=1)
    q_hi = jnp.concatenate([zpad, qs], axis=1)

    def tok_rows(width):
        return qb * nq + lax.broadcasted_iota(jnp.int32, (nq, width), 0)

    lane_row = lax.broadcasted_iota(jnp.int32, (nrow, LANES), 1)

    def lane_max(x):
        return functools.reduce(jnp.maximum, [x[:, i:i + LANES] for i in range(0, x.shape[1], LANES)])

    def rep_lanes(x, width):
        return jnp.concatenate([x] * (width // LANES), axis=1)

    def row_bcast(x):
        return jnp.broadcast_to(x, (x.shape[0], LANES))

    gt = gate_ref[0, 0]
    g_hi = gt.astype(BF16)
    g_lo = (gt - g_hi.astype(F32)).astype(BF16)
    gexp = _dot(g_hi, eg_ref[...]) + _dot(g_lo, eg_ref[...])

    def gate_rows(br):
        return jnp.concatenate(
            [gexp[:, (N_BRANCH * r + br) * LANES:(N_BRANCH * r + br + 1) * LANES] for r in range(REP)], axis=0)

    kvc = kvc_ref[0, 0]
    ncb = kvc.shape[0]
    sc = _dot_nt(q_lo, kvc).reshape(REP, nq, ncb)
    cidx = lax.broadcasted_iota(jnp.int32, (nq, ncb), 1)
    mask_c = (cidx * CMP_STRIDE + (CMP_LEN - 1)) <= tok_rows(ncb)
    sc = jnp.where(mask_c[None], sc, NEG_INF)
    mc = jnp.max(sc, axis=-1, keepdims=True)
    pc = jnp.where(mask_c[None], jnp.exp(sc - mc), 0.0)
    lc = jnp.sum(pc, axis=-1, keepdims=True)
    pc = pc / jnp.where(lc > 0.0, lc, 1.0)
    o_cmp = _dot(pc.reshape(nrow, ncb).astype(BF16), kvc)

    psum = pc[0] + pc[1] + pc[2] + pc[3]
    p_hi = psum.astype(BF16)
    p_lo = (psum - p_hi.astype(F32)).astype(BF16)
    ot = ot_ref[...]
    imp = _dot_nt(ot, p_hi) + _dot_nt(ot, p_lo)
    nblk = imp.shape[0]
    jj = lax.broadcasted_iota(jnp.int32, (nblk, nq), 0)
    lane = lax.broadcasted_iota(jnp.int32, (nblk, nq), 1)
    qblk = jnp.right_shift(qb * nq + lane, int(np.log2(SEL_BLOCK)))
    valid = jj <= qblk
    forced = (jj == 0) | (jj == qblk) | (jj == qblk - 1)
    imp = jnp.where(valid, imp + jnp.where(forced, FORCE_BONUS, 0.0), NEG_INF)
    rank = jnp.zeros((nblk, nq), F32)
    for i in range(nblk):
        row = imp[i:i + 1, :]
        beats = (row > imp) | ((row == imp) & (jj > i))
        rank = rank + jnp.where(beats, 1.0, 0.0)
    sel_t = jnp.where(rank < float(SEL_TOP_K), 1.0, 0.0)
    sel_t = jnp.concatenate(
        [sel_t, jnp.ones((HEAD_DIM - nblk, nq), F32), jnp.zeros((LANES - HEAD_DIM, nq), F32)], axis=0)
    selb = ((sel_t.T[:, :HEAD_DIM] - 1.0) * -NEG_INF).astype(BF16)
    q_sel = jnp.concatenate([qs, jnp.concatenate([selb] * REP, axis=0)], axis=1)

    n_win = WINDOW // KEY_CHUNK + 1
    diag = qb // (KEY_CHUNK // nq)
    k0_win, ok_win = [], []
    for i in range(n_win):
        cw = diag - (n_win - 1) + i
        k0_win.append(pl.multiple_of(jnp.maximum(cw, 0) * KEY_CHUNK, KEY_CHUNK))
        kpos = cw * KEY_CHUNK + lax.broadcasted_iota(jnp.int32, (nq, KEY_CHUNK), 1)
        tq = tok_rows(KEY_CHUNK)
        ok_win.append((kpos <= tq) & (kpos > tq - WINDOW) & (kpos >= 0))
    half = nrow // 2
    acc_halves = []
    for hh in range(2):
        qh = q_hi[hh * half:(hh + 1) * half]
        s_win = []
        for k0, ok in zip(k0_win, ok_win):
            s = _dot_nt(qh, kk_ref[0, 0, pl.ds(k0, KEY_CHUNK), :]).reshape(REP // 2, nq, KEY_CHUNK)
            s_win.append(jnp.where(ok[None], s, NEG_INF).reshape(half, KEY_CHUNK))
        m_w = row_bcast(jnp.max(functools.reduce(jnp.maximum, [lane_max(s) for s in s_win]), axis=1, keepdims=True))
        acc_halves.append(functools.reduce(jnp.add, [
            _dot(jnp.exp(s - rep_lanes(m_w, KEY_CHUNK)).astype(BF16), vw_ref[0, 0, pl.ds(k0, KEY_CHUNK), :])
            for s, k0 in zip(s_win, k0_win)]))
    acc_w = jnp.concatenate(acc_halves, axis=0)
    o_hi = gate_rows(2) * acc_w / pltpu.roll(acc_w, HEAD_DIM, axis=1) + gate_rows(0) * o_cmp

    n_sel = qb // (SEL_CHUNK // nq) + 1

    def sel_scores(c, causal):
        k0 = pl.multiple_of(c * SEL_CHUNK, SEL_CHUNK)
        s = _dot_nt(q_sel, kx_ref[0, 0, pl.ds(k0, SEL_CHUNK), :])
        if causal:
            kpos = k0 + lax.broadcasted_iota(jnp.int32, (nq, SEL_CHUNK), 1)
            ok = kpos <= tok_rows(SEL_CHUNK)
            s = jnp.where(ok[None], s.reshape(REP, nq, SEL_CHUNK), NEG_INF).reshape(nrow, SEL_CHUNK)
        s_ref[:, pl.ds(k0, SEL_CHUNK)] = s
        return lane_max(s)

    def pass1(c, m_run):
        return jnp.maximum(m_run, sel_scores(c, False))

    m_run = lax.fori_loop(0, n_sel - 1, pass1, jnp.full((nrow, LANES), NEG_INF, F32))
    m_run = jnp.maximum(m_run, sel_scores(n_sel - 1, True))
    m_s = row_bcast(jnp.max(m_run, axis=1, keepdims=True))
    acc_ref[...] = jnp.zeros(acc_ref.shape, F32)

    def pass2(c, carry):
        k0 = pl.multiple_of(c * SEL_CHUNK, SEL_CHUNK)
        p = jnp.exp(s_ref[:, pl.ds(k0, SEL_CHUNK)] - rep_lanes(m_s, SEL_CHUNK))
        acc_ref[...] += _dot(p.astype(BF16), vs_ref[0, 0, pl.ds(k0, SEL_CHUNK), :])
        return carry

    lax.fori_loop(0, n_sel, pass2, 0)
    acc_s = acc_ref[...]
    o_lo = gate_rows(1) * acc_s / pltpu.roll(acc_s, HEAD_DIM, axis=1)

    t = jnp.where(lane_row < HEAD_DIM, o_lo, o_hi)
    u = t + pltpu.roll(t, HEAD_DIM, axis=1)
    lane_q = lax.broadcasted_iota(jnp.int32, (nq, LANES), 1)
    o_ref[0] = jnp.concatenate(
        [jnp.where(lane_q < HEAD_DIM, u[(2 * i) * nq:(2 * i + 1) * nq], u[(2 * i + 1) * nq:(2 * i + 2) * nq])
         for i in range(REP // 2)], axis=1)


def _attn_call(q, kx, kk, vs, vw, kvc, gate, overlap_t, gate_expand):
    nb, seq, _ = q.shape
    nq = seq // Q_BLOCK
    ncb = kvc.shape[2]
    per_bg = lambda rows: pl.BlockSpec((1, 1, rows, LANES), lambda b, g, i: (b, g, 0, 0))
    return pl.pallas_call(
        _attn_kernel,
        grid=(nb, KV_GROUPS, nq),
        in_specs=[
            pl.BlockSpec((1, Q_BLOCK, REP * HEAD_DIM), lambda b, g, i: (b, i, g)),
            per_bg(seq), per_bg(seq), per_bg(seq), per_bg(seq), per_bg(ncb),
            pl.BlockSpec((1, 1, Q_BLOCK, LANES), lambda b, g, i: (b, g, i, 0)),
            pl.BlockSpec(overlap_t.shape, lambda b, g, i: (0, 0)),
            pl.BlockSpec(gate_expand.shape, lambda b, g, i: (0, 0)),
        ],
        out_specs=pl.BlockSpec((1, Q_BLOCK, REP * HEAD_DIM), lambda b, g, i: (b, i, g)),
        out_shape=jax.ShapeDtypeStruct((nb, seq, D_NSA), F32),
        scratch_shapes=[
            pltpu.VMEM((REP * Q_BLOCK, seq), F32),
            pltpu.VMEM((REP * Q_BLOCK, LANES), F32),
        ],
        compiler_params=pltpu.CompilerParams(
            dimension_semantics=("arbitrary", "arbitrary", "arbitrary"),
            vmem_limit_bytes=48 * 1024 * 1024),
        name="nsa_attention",
    )(q, kx, kk, vs, vw, kvc, gate, overlap_t, gate_expand)


def _lru_kernel(zx_ref, zy_ref, cw_ref, cb_ref, w_ref, bias_ref, lam_ref, o_ref, hist_ref, h_ref):
    t = pl.program_id(1)
    tl = zx_ref.shape[1]

    @pl.when(t == 0)
    def _():
        hist_ref[:HIST, :] = jnp.zeros((HIST, D_LRU), F32)
        h_ref[...] = jnp.zeros(h_ref.shape, F32)

    x = zx_ref[0]
    hist_ref[HIST:, :] = x
    xc = cb_ref[...] + x * cw_ref[LRU_CONV_W - 1:LRU_CONV_W, :]
    for k in range(LRU_CONV_W - 1):
        off = HIST - (LRU_CONV_W - 1) + k
        xc = xc + hist_ref[pl.ds(off, tl), :] * cw_ref[k:k + 1, :]
    hist_ref[:HIST, :] = x[tl - HIST:, :]

    y = _dot(xc.astype(BF16), w_ref[...])
    r = _sigmoid(y[:, :D_LRU] + bias_ref[0:1, :])
    i = _sigmoid(y[:, D_LRU:] + bias_ref[1:2, :])
    lam = lam_ref[...]
    log_sig = -(jnp.maximum(-lam, 0.0) + jnp.log1p(jnp.exp(-jnp.abs(lam))))
    log_a = LRU_C * r * log_sig
    a = jnp.exp(log_a)
    u = jnp.sqrt(-jnp.tanh(log_a) * (a * a + 1.0)) * (i * xc)

    groups = tl // HIST
    a3 = a.reshape(groups, HIST, D_LRU)
    u3 = u.reshape(groups, HIST, D_LRU)
    sub = lax.broadcasted_iota(jnp.int32, (groups, HIST, D_LRU), 1)
    d = 1
    while d < HIST:
        ok = sub >= d
        a_sh = pltpu.roll(a3, d, axis=1)
        u_sh = pltpu.roll(u3, d, axis=1)
        u3 = jnp.where(ok, a3 * u_sh + u3, u3)
        a3 = jnp.where(ok, a3 * a_sh, a3)
        d *= 2
    hb = h_ref[...]
    hs = []
    for k in range(groups):
        hk = a3[k] * hb + u3[k]
        hs.append(hk)
        hb = jnp.broadcast_to(hk[HIST - 1:HIST, :], (HIST, D_LRU))
    h_ref[...] = hb
    o_ref[0] = jnp.concatenate(hs, axis=0) * _gelu_tanh(zy_ref[0])


def _lru_call(zx, zy, conv_w, conv_b, w_ax, bias_ax, lam, l):
    nb, seq, _ = zx.shape
    tl = LRU_TILE
    row = pl.BlockSpec((1, tl, D_LRU), lambda b, t: (b, t, 0))
    return pl.pallas_call(
        _lru_kernel,
        grid=(nb, seq // tl),
        in_specs=[
            row, row,
            pl.BlockSpec((None, LRU_CONV_W, D_LRU), lambda b, t: (l, 0, 0)),
            pl.BlockSpec((None, 1, D_LRU), lambda b, t: (l, 0, 0)),
            pl.BlockSpec((None, D_LRU, 2 * D_LRU), lambda b, t: (l, 0, 0)),
            pl.BlockSpec((None, 2, D_LRU), lambda b, t: (l, 0, 0)),
            pl.BlockSpec((None, 1, D_LRU), lambda b, t: (l, 0, 0)),
        ],
        out_specs=row,
        out_shape=jax.ShapeDtypeStruct((nb, seq, D_LRU), F32),
        scratch_shapes=[pltpu.VMEM((HIST + tl, D_LRU), F32), pltpu.VMEM((HIST, D_LRU), F32)],
        compiler_params=pltpu.CompilerParams(dimension_semantics=("arbitrary", "arbitrary")),
        name="rglru",
    )(zx, zy, conv_w, conv_b, w_ax, bias_ax, lam)


def _ffn_kernel(x_ref, on_ref, ol_ref, mod_ref, gn_ref, gl_ref, wo_ref, g_ref, wg_ref, wu_ref, cw_ref, cb_ref,
                wd_ref, fg_ref, o_ref, hist_ref, acc_ref, *, final_norm):
    t = pl.program_id(1)
    tm = x_ref.shape[1]

    @pl.when(t == 0)
    def _():
        hist_ref[...] = jnp.zeros(hist_ref.shape, F32)

    a = (_rms(on_ref[0]) * gn_ref[...]).astype(BF16)
    b = (_rms(ol_ref[0]) * gl_ref[...]).astype(BF16)
    x = x_ref[0] + mod_ref[2:3, :] * (_dot(a, wo_ref[:D_NSA, :]) + _dot(b, wo_ref[D_NSA:, :]))
    h = (_rms(x) * g_ref[...]) * (1.0 + mod_ref[4:5, :]) + mod_ref[3:4, :]
    hb = h.astype(BF16)
    for j in range(D_FF // FF_CHUNK):
        cs = slice(j * FF_CHUNK, (j + 1) * FF_CHUNK)
        gp = _dot(hb, wg_ref[:, cs])
        ge = jnp.concatenate([hist_ref[:, cs], gp], axis=0)
        hist_ref[:, cs] = gp[tm - HIST:, :]
        gate = cb_ref[:, cs] + gp * cw_ref[FFN_CONV_W - 1:FFN_CONV_W, cs]
        for k in range(FFN_CONV_W - 1):
            off = HIST - (FFN_CONV_W - 1) + k
            gate = gate + ge[off:off + tm, :] * cw_ref[k:k + 1, cs]
        act = (gate * _sigmoid(gate)) * _dot(hb, wu_ref[:, cs])
        contrib = _dot(act.astype(BF16), wd_ref[cs, :])
        if j == 0:
            acc_ref[...] = contrib
        else:
            acc_ref[...] += contrib
    y = x + mod_ref[5:6, :] * acc_ref[...]
    if final_norm:
        y = _rms(y) * fg_ref[...]
    o_ref[0] = y


def _ffn_call(x, o_nsa, o_lru, mod, g_nsa, g_lru, w_out, norm_g, w_gate, w_up, conv_w, conv_b, w_down, final_g,
              l, final_norm):
    nb, seq, _ = x.shape
    tm = ROW_TILE
    row = pl.BlockSpec((1, tm, D_MODEL), lambda b, t: (b, t, 0))
    half_row = pl.BlockSpec((1, tm, D_NSA), lambda b, t: (b, t, 0))
    once = dict(pipeline_mode=pl.Buffered(1))
    return pl.pallas_call(
        functools.partial(_ffn_kernel, final_norm=final_norm),
        grid=(nb, seq // tm),
        in_specs=[
            row, half_row, half_row,
            pl.BlockSpec((None, None, 6, D_MODEL), lambda b, t: (l, b, 0, 0)),
            pl.BlockSpec((None, 1, D_NSA), lambda b, t: (l, 0, 0)),
            pl.BlockSpec((None, 1, D_LRU), lambda b, t: (l, 0, 0)),
            pl.BlockSpec((None, D_MODEL, D_MODEL), lambda b, t: (l, 0, 0), **once),
            pl.BlockSpec((None, 1, D_MODEL), lambda b, t: (l, 0, 0)),
            pl.BlockSpec((None, D_MODEL, D_FF), lambda b, t: (l, 0, 0), **once),
            pl.BlockSpec((None, D_MODEL, D_FF), lambda b, t: (l, 0, 0), **once),
            pl.BlockSpec((None, FFN_CONV_W, D_FF), lambda b, t: (l, 0, 0)),
            pl.BlockSpec((None, 1, D_FF), lambda b, t: (l, 0, 0)),
            pl.BlockSpec((None, D_FF, D_MODEL), lambda b, t: (l, 0, 0), **once),
            pl.BlockSpec((1, D_MODEL), lambda b, t: (0, 0)),
        ],
        out_specs=row,
        out_shape=jax.ShapeDtypeStruct(x.shape, F32),
        scratch_shapes=[pltpu.VMEM((HIST, D_FF), F32), pltpu.VMEM((tm, D_MODEL), F32)],
        compiler_params=pltpu.CompilerParams(
            dimension_semantics=("arbitrary", "arbitrary"), vmem_limit_bytes=56 * 1024 * 1024),
        name="conv_ffn",
    )(x, o_nsa, o_lru, mod, g_nsa, g_lru, w_out, norm_g, w_gate, w_up, conv_w, conv_b, w_down, final_g)


def _in_perm():
    zero = IN_COLS
    q0, kc0, vc0 = 0, D_NSA, D_NSA + KV_W
    ks0, vs0, kw0, vw0 = (D_NSA + i * KV_W for i in range(2, 6))
    zg0 = D_NSA + 6 * KV_W
    zx0 = zg0 + N_BRANCH * HEADS
    zy0 = zx0 + D_LRU
    cols = list(range(q0, q0 + D_NSA)) + list(range(kc0, kc0 + KV_W)) + list(range(vc0, vc0 + KV_W))
    for a0, b0 in ((ks0, kw0), (vs0, vw0)):
        for g in range(KV_GROUPS):
            cols += list(range(a0 + g * HEAD_DIM, a0 + (g + 1) * HEAD_DIM))
            cols += list(range(b0 + g * HEAD_DIM, b0 + (g + 1) * HEAD_DIM))
    cols += list(range(zx0, zx0 + D_LRU)) + list(range(zy0, zy0 + D_LRU))
    per_g = N_BRANCH * REP
    for g in range(KV_GROUPS):
        cols += list(range(zg0 + g * per_g, zg0 + (g + 1) * per_g)) + [zero] * (LANES - per_g)
    assert len(cols) == IN_COLS_P
    return np.asarray(cols, np.int32)


def _prep_in(w_in, gate_b):
    perm = _in_perm()
    cuts = [0] + [i for i in range(1, len(perm)) if perm[i] != perm[i - 1] + 1 and not (perm[i] == perm[i - 1] == IN_COLS)]
    parts = []
    for a, b in zip(cuts, cuts[1:] + [len(perm)]):
        if perm[a] == IN_COLS:
            parts.append(jnp.zeros(w_in.shape[:-1] + (b - a,), BF16))
        else:
            parts.append(w_in[..., int(perm[a]):int(perm[a]) + (b - a)].astype(BF16))
    w_p = jnp.concatenate(parts, axis=-1)
    per_g = N_BRANCH * REP
    gb = gate_b.reshape(DEPTH, KV_GROUPS, per_g)
    gb = jnp.pad(gb, ((0, 0), (0, 0), (0, LANES - per_g))).reshape(DEPTH, 1, KV_GROUPS * LANES)
    return w_p, gb


def _prep_cmp(cmp_pos, cmp_w1, cmp_b1, cmp_w2, cmp_b2):
    half = CMP_LEN // 2
    pos = cmp_pos.reshape(DEPTH, 2, 2, half, 1, HEAD_DIM)
    pos = jnp.broadcast_to(pos, (DEPTH, 2, 2, half, KV_GROUPS, HEAD_DIM))
    pos_p = pos.reshape(DEPTH, 4, half * KV_W)
    w1 = cmp_w1.reshape(DEPTH, 2, 2, half, HEAD_DIM, CMP_HIDDEN)
    w1 = w1.astype(BF16)
    zero = jnp.zeros_like(w1)
    assert KV_GROUPS == 2
    w1_p = jnp.stack([jnp.concatenate([w1, zero], axis=-1), jnp.concatenate([zero, w1], axis=-1)], axis=4)
    w1_p = w1_p.reshape(DEPTH, 4, half * KV_W, KV_GROUPS * CMP_HIDDEN)
    b1_p = jnp.tile(cmp_b1, (1, 1, KV_GROUPS))
    zeros = jnp.zeros((DEPTH, CMP_HIDDEN, HEAD_DIM), cmp_w2.dtype)
    w2_p = jnp.concatenate([
        jnp.concatenate([cmp_w2[:, 0], zeros], axis=-1),
        jnp.concatenate([zeros, cmp_w2[:, 1]], axis=-1)], axis=1).astype(BF16)
    b2_p = cmp_b2.reshape(DEPTH, 1, 2 * HEAD_DIM)
    return pos_p, w1_p, b1_p, w2_p, b2_p


def _prep_lru(lru_wa, lru_wx, lru_ba, lru_bx):
    eye = jnp.eye(LRU_BLOCKS, dtype=lru_wa.dtype)
    dense = lambda w: jnp.einsum('lncd,nm->lncmd', w, eye).reshape(DEPTH, D_LRU, D_LRU)
    w_ax = jnp.concatenate([dense(lru_wa), dense(lru_wx)], axis=-1).astype(BF16)
    bias_ax = jnp.stack([lru_ba, lru_bx], axis=1)
    return w_ax, bias_ax


def _attn_consts(seq):
    nblk = seq // SEL_BLOCK
    ncb_p = seq // CMP_STRIDE
    cstart = np.arange(ncb_p) * CMP_STRIDE
    js = np.arange(nblk) * SEL_BLOCK
    overlap_t = ((cstart[None, :] < js[:, None] + SEL_BLOCK) & (cstart[None, :] + CMP_LEN > js[:, None]))
    n_gate = N_BRANCH * REP
    gate_expand = np.zeros((LANES, n_gate * LANES), np.float32)
    for i in range(n_gate):
        gate_expand[i, i * LANES:(i + 1) * LANES] = 1.0
    return jnp.asarray(overlap_t.astype(np.float32), BF16), jnp.asarray(gate_expand, BF16)


def kernel(x, c, ada_w, ada_b, mix_norm_g, ffn_norm_g, w_in, nsa_gate_b, cmp_pos, cmp_w1, cmp_b1, cmp_w2, cmp_b2, lru_conv_w, lru_conv_b, lru_wa, lru_ba, lru_wx, lru_bx, lru_lambda, nsa_out_norm_g, lru_out_norm_g, w_out, ffn_w_gate, ffn_w_up, ffn_conv_w, ffn_conv_b, ffn_w_down, final_norm_g):
    nb, seq, _ = x.shape
    mod = _mod_call(c, ada_w, ada_b)
    w_in_p, gate_b_p = _prep_in(w_in, nsa_gate_b)
    pos_p, w1_p, b1_p, w2_p, b2_p = _prep_cmp(cmp_pos, cmp_w1, cmp_b1, cmp_w2, cmp_b2)
    w_ax, bias_ax = _prep_lru(lru_wa, lru_wx, lru_ba, lru_bx)
    overlap_t, gate_expand = _attn_consts(seq)
    w_out_b = w_out.astype(BF16)
    wg_b, wu_b, wd_b = ffn_w_gate.astype(BF16), ffn_w_up.astype(BF16), ffn_w_down.astype(BF16)
    row3 = lambda a: a.reshape(DEPTH, 1, a.shape[-1])
    mix_g, ffn_g = row3(mix_norm_g), row3(ffn_norm_g)
    nsa_g, lru_g = row3(nsa_out_norm_g), row3(lru_out_norm_g)
    lru_cb, lam, ffn_cb = row3(lru_conv_b), row3(lru_lambda), row3(ffn_conv_b)
    final_g = final_norm_g.reshape(1, D_MODEL)

    for l in range(DEPTH):
        q, kc, vc, kx, kk, vs, vw, zx, zy, gate = _in_call(x, mod, mix_g, w_in_p, gate_b_p, l)
        kvc = _cmp_call(kc, vc, pos_p, w1_p, b1_p, w2_p, b2_p, l)
        o_nsa = _attn_call(q, kx, kk, vs, vw, kvc, gate, overlap_t, gate_expand)
        o_lru = _lru_call(zx, zy, lru_conv_w, lru_cb, w_ax, bias_ax, lam, l)
        x = _ffn_call(x, o_nsa, o_lru, mod, nsa_g, lru_g, w_out_b, ffn_g, wg_b, wu_b, ffn_conv_w, ffn_cb, wd_b,
                      final_g, l, l == DEPTH - 1)
    return x
```

```python
import functools

import numpy as np
import jax
import jax.numpy as jnp
from jax import lax
from jax.experimental import pallas as pl
from jax.experimental.pallas import tpu as pltpu

F32 = jnp.float32
BF16 = jnp.bfloat16

D_MODEL = 1024
DEPTH = 4
D_NSA = 512
D_LRU = 512
HEADS = 8
HEAD_DIM = 64
KV_GROUPS = 2
REP = HEADS // KV_GROUPS
N_BRANCH = 3
CMP_LEN = 32
CMP_STRIDE = 16
CMP_HIDDEN = 4 * HEAD_DIM
SEL_BLOCK = 64
SEL_TOP_K = 8
WINDOW = 512
Q_BLOCK = 128
LRU_BLOCKS = 8
LRU_BLOCK_W = D_LRU // LRU_BLOCKS
LRU_CONV_W = 4
LRU_C = 8.0
D_FF = 2816
FFN_CONV_W = 3
NORM_EPS = 1e-6
NEG_INF = -1e30
FORCE_BONUS = 1e4

KV_W = KV_GROUPS * HEAD_DIM
IN_COLS = D_NSA + 6 * KV_W + N_BRANCH * HEADS + 2 * D_LRU
LANES = 128
KEY_CHUNK = 256
SEL_CHUNK = 512
ROW_TILE = 512
LRU_TILE = 256
FF_CHUNK = 256
HIST = 8

C_Q = 0
C_KC = C_Q + D_NSA
C_VC = C_KC + KV_W
C_KK = C_VC + KV_W
C_VV = C_KK + KV_GROUPS * LANES
C_ZX = C_VV + KV_GROUPS * LANES
C_ZY = C_ZX + D_LRU
C_GATE = C_ZY + D_LRU
IN_COLS_P = C_GATE + KV_GROUPS * LANES


def _dot(a, b):
    return jnp.dot(a, b, preferred_element_type=F32)


def _dot_nt(a, b):
    return lax.dot_general(a, b, (((1,), (1,)), ((), ())), preferred_element_type=F32)


def _sigmoid(x):
    return 1.0 / (1.0 + jnp.exp(-x))


def _gelu_tanh(x):
    c = np.float32(np.sqrt(2.0 / np.pi))
    return x * (0.5 * (1.0 + jnp.tanh(c * (x + 0.044715 * (x * x * x)))))


def _rms(x):
    return x * lax.rsqrt(jnp.mean(x * x, axis=-1, keepdims=True) + NORM_EPS)


def _mod_kernel(c_ref, w_ref, b_ref, o_ref):
    c = c_ref[...]
    ca = (c * _sigmoid(c)).astype(BF16)
    o_ref[...] = _dot(ca, w_ref[...].astype(BF16)) + b_ref[...]


def _mod_call(c, ada_w, ada_b):
    nb = c.shape[0]
    out = pl.pallas_call(
        _mod_kernel,
        grid=(DEPTH, 6),
        in_specs=[
            pl.BlockSpec((nb, D_MODEL), lambda l, j: (0, 0)),
            pl.BlockSpec((None, D_MODEL, D_MODEL), lambda l, j: (l, 0, j)),
            pl.BlockSpec((None, 1, D_MODEL), lambda l, j: (l, 0, j)),
        ],
        out_specs=pl.BlockSpec((None, nb, D_MODEL), lambda l, j: (l, 0, j)),
        out_shape=jax.ShapeDtypeStruct((DEPTH, nb, 6 * D_MODEL), F32),
        name="adaln_mod",
    )(c, ada_w, ada_b.reshape(DEPTH, 1, 6 * D_MODEL))
    return out.reshape(DEPTH, nb, 6, D_MODEL)


def _in_kernel(x_ref, mod_ref, g_ref, w_ref, gb_ref,
               q_ref, kc_ref, vc_ref, kx_ref, kk_ref, vs_ref, vw_ref, zx_ref, zy_ref, gate_ref, tok_ref):
    x = x_ref[0]
    tm = x.shape[0]
    lane = lax.broadcasted_iota(jnp.int32, (tm, LANES), 1)
    tok = pl.program_id(1) * tm + lax.broadcasted_iota(jnp.int32, (tm, LANES), 0)
    blk_hot = jnp.where(lane - HEAD_DIM == jnp.right_shift(tok, int(np.log2(SEL_BLOCK))), 1.0, 0.0)
    h = (_rms(x) * g_ref[...]) * (1.0 + mod_ref[1:2, :]) + mod_ref[0:1, :]
    hb = h.astype(BF16)

    def seg(start, width):
        return _dot(hb, w_ref[:, start:start + width])

    q_ref[0] = (seg(C_Q, D_NSA) * (HEAD_DIM ** -0.5)).astype(BF16)
    for out_ref, start in ((kc_ref, C_KC), (vc_ref, C_VC)):
        tok_ref[...] = seg(start, KV_W)
        for i in range(CMP_STRIDE):
            out_ref[0, :, i * KV_W:(i + 1) * KV_W] = tok_ref[pl.ds(i, tm // CMP_STRIDE, stride=CMP_STRIDE), :]
    for g in range(KV_GROUPS):
        kk = seg(C_KK + g * LANES, LANES)
        kk_ref[0, g] = kk.astype(BF16)
        kx_ref[0, g] = jnp.where(lane < HEAD_DIM, kk, blk_hot).astype(BF16)
        vv = seg(C_VV + g * LANES, LANES)
        vs_ref[0, g] = jnp.where(lane < HEAD_DIM, vv, 1.0).astype(BF16)
        vw_ref[0, g] = jnp.where(lane < HEAD_DIM, 1.0, vv).astype(BF16)
        gate_ref[0, g] = _sigmoid(seg(C_GATE + g * LANES, LANES) + gb_ref[:, g * LANES:(g + 1) * LANES])
    zx_ref[0] = seg(C_ZX, D_LRU)
    zy_ref[0] = seg(C_ZY, D_LRU)


def _in_call(x, mod, norm_g, w_in_p, gate_b_p, l):
    nb, seq, _ = x.shape
    tm = ROW_TILE
    row = lambda width: pl.BlockSpec((1, tm, width), lambda b, t: (b, t, 0))
    grp = pl.BlockSpec((1, KV_GROUPS, tm, LANES), lambda b, t: (b, 0, t, 0))
    cmp_rows, cmp_width = tm // CMP_STRIDE, CMP_STRIDE * KV_W
    cmp_in = pl.BlockSpec((1, cmp_rows, cmp_width), lambda b, t: (b, t, 0))
    return pl.pallas_call(
        _in_kernel,
        grid=(nb, seq // tm),
        in_specs=[
            row(D_MODEL),
            pl.BlockSpec((None, None, 6, D_MODEL), lambda b, t: (l, b, 0, 0)),
            pl.BlockSpec((None, 1, D_MODEL), lambda b, t: (l, 0, 0)),
            pl.BlockSpec((None, D_MODEL, IN_COLS_P), lambda b, t: (l, 0, 0)),
            pl.BlockSpec((None, 1, KV_GROUPS * LANES), lambda b, t: (l, 0, 0)),
        ],
        out_specs=[row(D_NSA), cmp_in, cmp_in, grp, grp, grp, grp, row(D_LRU), row(D_LRU), grp],
        out_shape=[
            jax.ShapeDtypeStruct((nb, seq, D_NSA), BF16),
            jax.ShapeDtypeStruct((nb, seq // CMP_STRIDE, cmp_width), F32),
            jax.ShapeDtypeStruct((nb, seq // CMP_STRIDE, cmp_width), F32),
            jax.ShapeDtypeStruct((nb, KV_GROUPS, seq, LANES), BF16),
            jax.ShapeDtypeStruct((nb, KV_GROUPS, seq, LANES), BF16),
            jax.ShapeDtypeStruct((nb, KV_GROUPS, seq, LANES), BF16),
            jax.ShapeDtypeStruct((nb, KV_GROUPS, seq, LANES), BF16),
            jax.ShapeDtypeStruct((nb, seq, D_LRU), F32),
            jax.ShapeDtypeStruct((nb, seq, D_LRU), F32),
            jax.ShapeDtypeStruct((nb, KV_GROUPS, seq, LANES), F32),
        ],
        scratch_shapes=[pltpu.VMEM((tm, KV_W), F32)],
        compiler_params=pltpu.CompilerParams(
            dimension_semantics=("arbitrary", "arbitrary"), vmem_limit_bytes=48 * 1024 * 1024),
        name="mixer_in",
    )(x, mod, norm_g, w_in_p, gate_b_p)


def _cmp_kernel(kc_ref, vc_ref, pos_ref, w1_ref, b1_ref, w2_ref, b2_ref, o_ref):
    def hidden(x_ref, i):
        x = x_ref[0]
        top = (x + pos_ref[2 * i:2 * i + 1, :]).astype(BF16)
        bot = (x + pos_ref[2 * i + 1:2 * i + 2, :]).astype(BF16)
        a = _dot(top, w1_ref[2 * i])
        b = _dot(bot, w1_ref[2 * i + 1])
        nrow = b.shape[0]
        b = pltpu.roll(b, nrow - 1, axis=0)
        return _gelu_tanh(a + b + b1_ref[i:i + 1, :])

    hk = hidden(kc_ref, 0)
    hv = hidden(vc_ref, 1)
    for g in range(KV_GROUPS):
        sl = slice(g * CMP_HIDDEN, (g + 1) * CMP_HIDDEN)
        hg = jnp.concatenate([hk[:, sl], hv[:, sl]], axis=1).astype(BF16)
        o_ref[0, g] = (_dot(hg, w2_ref[...]) + b2_ref[...]).astype(BF16)


def _cmp_call(kc16, vc16, pos_p, w1_p, b1_p, w2_p, b2_p, l):
    nb, nrow, width = kc16.shape
    blk = pl.BlockSpec((1, nrow, width), lambda b: (b, 0, 0))
    return pl.pallas_call(
        _cmp_kernel,
        grid=(nb,),
        in_specs=[
            blk, blk,
            pl.BlockSpec((None, 4, width), lambda b: (l, 0, 0)),
            pl.BlockSpec((None, 4, width, KV_GROUPS * CMP_HIDDEN), lambda b: (l, 0, 0, 0)),
            pl.BlockSpec((None, 2, KV_GROUPS * CMP_HIDDEN), lambda b: (l, 0, 0)),
            pl.BlockSpec((None, 2 * CMP_HIDDEN, LANES), lambda b: (l, 0, 0)),
            pl.BlockSpec((None, 1, LANES), lambda b: (l, 0, 0)),
        ],
        out_specs=pl.BlockSpec((1, KV_GROUPS, nrow, LANES), lambda b: (b, 0, 0, 0)),
        out_shape=jax.ShapeDtypeStruct((nb, KV_GROUPS, nrow, LANES), BF16),
        compiler_params=pltpu.CompilerParams(
            dimension_semantics=("arbitrary",), vmem_limit_bytes=48 * 1024 * 1024),
        name="nsa_compress",
    )(kc16, vc16, pos_p, w1_p, b1_p, w2_p, b2_p)


def _attn_body(n_sel, q_ref, kx_ref, kk_ref, vs_ref, vw_ref, kvc_ref, gate_ref, ot_ref, o_ref):
    qb = pl.program_id(2)
    nq = Q_BLOCK
    nrow = REP * nq
    q = q_ref[0]
    qs = jnp.concatenate([q[:, r * HEAD_DIM:(r + 1) * HEAD_DIM] for r in range(REP)], axis=0)
    zpad = jnp.zeros_like(qs)
    q_lo = jnp.concatenate([qs, zpad], axis=1)
    q_hi = jnp.concatenate([zpad, qs], axis=1)

    def tok_rows(width):
        return qb * nq + lax.broadcasted_iota(jnp.int32, (nq, width), 0)

    lane_row = lax.broadcasted_iota(jnp.int32, (nrow, LANES), 1)

    def lane_max(x):
        return functools.reduce(jnp.maximum, [x[:, i:i + LANES] for i in range(0, x.shape[1], LANES)])

    def rep_lanes(x, width):
        return jnp.concatenate([x] * (width // LANES), axis=1)

    def row_bcast(x):
        return jnp.broadcast_to(x, (x.shape[0], LANES))

    gt = gate_ref[0, 0]

    def gate_rows(br):
        return jnp.concatenate(
            [jnp.broadcast_to(gt[:, N_BRANCH * r + br:N_BRANCH * r + br + 1], (nq, LANES)) for r in range(REP)],
            axis=0)

    kvc = kvc_ref[0, 0]
    ncb = kvc.shape[0]
    sc = _dot_nt(q_lo, kvc).reshape(REP, nq, ncb)
    cidx = lax.broadcasted_iota(jnp.int32, (nq, ncb), 1)
    mask_c = (cidx * CMP_STRIDE + (CMP_LEN - 1)) <= tok_rows(ncb)
    sc = jnp.where(mask_c[None], sc, NEG_INF)
    mc = jnp.max(sc, axis=-1, keepdims=True)
    pc = jnp.where(mask_c[None], jnp.exp(sc - mc), 0.0)
    lc = jnp.sum(pc, axis=-1, keepdims=True)
    pc = pc / jnp.where(lc > 0.0, lc, 1.0)
    o_cmp = _dot(pc.reshape(nrow, ncb).astype(BF16), kvc)

    psum = pc[0] + pc[1] + pc[2] + pc[3]
    p_hi = psum.astype(BF16)
    p_lo = (psum - p_hi.astype(F32)).astype(BF16)
    ot = ot_ref[...]
    imp = _dot_nt(ot, p_hi) + _dot_nt(ot, p_lo)
    nblk = imp.shape[0]
    jj = lax.broadcasted_iota(jnp.int32, (nblk, nq), 0)
    lane = lax.broadcasted_iota(jnp.int32, (nblk, nq), 1)
    qblk = jnp.right_shift(qb * nq + lane, int(np.log2(SEL_BLOCK)))
    valid = jj <= qblk
    forced = (jj == 0) | (jj == qblk) | (jj == qblk - 1)
    imp = jnp.where(valid, imp + jnp.where(forced, FORCE_BONUS, 0.0), NEG_INF)
    rank = jnp.zeros((nblk, nq), F32)
    for i in range(nblk):
        row = imp[i:i + 1, :]
        beats = (row > imp) | ((row == imp) & (jj > i))
        rank = rank + jnp.where(beats, 1.0, 0.0)
    sel_t = jnp.where(rank < float(SEL_TOP_K), 1.0, 0.0)
    sel_t = jnp.concatenate(
        [sel_t, jnp.ones((HEAD_DIM - nblk, nq), F32), jnp.zeros((LANES - HEAD_DIM, nq), F32)], axis=0)
    selb = ((sel_t.T[:, :HEAD_DIM] - 1.0) * -NEG_INF).astype(BF16)
    q_sel = jnp.concatenate([qs, jnp.concatenate([selb] * REP, axis=0)], axis=1)

    n_win = WINDOW // KEY_CHUNK + 1
    diag = qb // (KEY_CHUNK // nq)
    k0_win, ok_win = [], []
    for i in range(n_win):
        cw = diag - (n_win - 1) + i
        k0_win.append(pl.multiple_of(jnp.maximum(cw, 0) * KEY_CHUNK, KEY_CHUNK))
        kpos = cw * KEY_CHUNK + lax.broadcasted_iota(jnp.int32, (nq, KEY_CHUNK), 1)
        tq = tok_rows(KEY_CHUNK)
        ok_win.append((kpos <= tq) & (kpos > tq - WINDOW) & (kpos >= 0))
    half = nrow // 2
    acc_halves = []
    for hh in range(2):
        qh = q_hi[hh * half:(hh + 1) * half]
        s_win = []
        for k0, ok in zip(k0_win, ok_win):
            s = _dot_nt(qh, kk_ref[0, 0, pl.ds(k0, KEY_CHUNK), :]).reshape(REP // 2, nq, KEY_CHUNK)
            s_win.append(jnp.where(ok[None], s, NEG_INF).reshape(half, KEY_CHUNK))
        m_w = row_bcast(jnp.max(functools.reduce(jnp.maximum, [lane_max(s) for s in s_win]), axis=1, keepdims=True))
        acc_halves.append(functools.reduce(jnp.add, [
            _dot(jnp.exp(s - rep_lanes(m_w, KEY_CHUNK)).astype(BF16), vw_ref[0, 0, pl.ds(k0, KEY_CHUNK), :])
            for s, k0 in zip(s_win, k0_win)]))
    acc_w = jnp.concatenate(acc_halves, axis=0)
    o_hi = gate_rows(2) * acc_w / pltpu.roll(acc_w, HEAD_DIM, axis=1) + gate_rows(0) * o_cmp

    s_sel = []
    for c in range(n_sel):
        s = _dot_nt(q_sel, kx_ref[0, 0, c * SEL_CHUNK:(c + 1) * SEL_CHUNK, :])
        if c == n_sel - 1:
            kpos = c * SEL_CHUNK + lax.broadcasted_iota(jnp.int32, (nq, SEL_CHUNK), 1)
            ok = kpos <= tok_rows(SEL_CHUNK)
            s = jnp.where(ok[None], s.reshape(REP, nq, SEL_CHUNK), NEG_INF).reshape(nrow, SEL_CHUNK)
        s_sel.append(s)
    m_s = row_bcast(jnp.max(functools.reduce(jnp.maximum, [lane_max(s) for s in s_sel]), axis=1, keepdims=True))
    acc_s = functools.reduce(jnp.add, [
        _dot(jnp.exp(s - rep_lanes(m_s, SEL_CHUNK)).astype(BF16), vs_ref[0, 0, c * SEL_CHUNK:(c + 1) * SEL_CHUNK, :])
        for c, s in enumerate(s_sel)])
    o_lo = gate_rows(1) * acc_s / pltpu.roll(acc_s, HEAD_DIM, axis=1)

    t = jnp.where(lane_row < HEAD_DIM, o_lo, o_hi)
    u = t + pltpu.roll(t, HEAD_DIM, axis=1)
    lane_q = lax.broadcasted_iota(jnp.int32, (nq, LANES), 1)
    o_ref[0] = jnp.concatenate(
        [jnp.where(lane_q < HEAD_DIM, u[(2 * i) * nq:(2 * i + 1) * nq], u[(2 * i + 1) * nq:(2 * i + 2) * nq])
         for i in range(REP // 2)], axis=1)


def _attn_kernel(*refs):
    chunks = pl.program_id(2) // (SEL_CHUNK // Q_BLOCK) + 1
    for n_sel in range(1, refs[1].shape[2] // SEL_CHUNK + 1):
        pl.when(chunks == n_sel)(functools.partial(_attn_body, n_sel, *refs))


def _attn_call(q, kx, kk, vs, vw, kvc, gate, overlap_t):
    nb, seq, _ = q.shape
    nq = seq // Q_BLOCK
    ncb = kvc.shape[2]
    per_bg = lambda rows: pl.BlockSpec((1, 1, rows, LANES), lambda b, g, i: (b, g, 0, 0))
    return pl.pallas_call(
        _attn_kernel,
        grid=(nb, KV_GROUPS, nq),
        in_specs=[
            pl.BlockSpec((1, Q_BLOCK, REP * HEAD_DIM), lambda b, g, i: (b, i, g)),
            per_bg(seq), per_bg(seq), per_bg(seq), per_bg(seq), per_bg(ncb),
            pl.BlockSpec((1, 1, Q_BLOCK, LANES), lambda b, g, i: (b, g, i, 0)),
            pl.BlockSpec(overlap_t.shape, lambda b, g, i: (0, 0)),
        ],
        out_specs=pl.BlockSpec((1, Q_BLOCK, REP * HEAD_DIM), lambda b, g, i: (b, i, g)),
        out_shape=jax.ShapeDtypeStruct((nb, seq, D_NSA), F32),
        compiler_params=pltpu.CompilerParams(
            dimension_semantics=("arbitrary", "arbitrary", "arbitrary"),
            vmem_limit_bytes=48 * 1024 * 1024),
        name="nsa_attention",
    )(q, kx, kk, vs, vw, kvc, gate, overlap_t)


def _lru_kernel(zx_ref, zy_ref, cw_ref, cb_ref, w_ref, bias_ref, lam_ref, o_ref, hist_ref, h_ref):
    t = pl.program_id(1)
    tl = zx_ref.shape[1]

    @pl.when(t == 0)
    def _():
        hist_ref[:HIST, :] = jnp.zeros((HIST, D_LRU), F32)
        h_ref[...] = jnp.zeros(h_ref.shape, F32)

    x = zx_ref[0]
    hist_ref[HIST:, :] = x
    xc = cb_ref[...] + x * cw_ref[LRU_CONV_W - 1:LRU_CONV_W, :]
    for k in range(LRU_CONV_W - 1):
        off = HIST - (LRU_CONV_W - 1) + k
        xc = xc + hist_ref[pl.ds(off, tl), :] * cw_ref[k:k + 1, :]
    hist_ref[:HIST, :] = x[tl - HIST:, :]

    y = _dot(xc.astype(BF16), w_ref[...])
    r = _sigmoid(y[:, :D_LRU] + bias_ref[0:1, :])
    i = _sigmoid(y[:, D_LRU:] + bias_ref[1:2, :])
    lam = lam_ref[...]
    log_sig = -(jnp.maximum(-lam, 0.0) + jnp.log1p(jnp.exp(-jnp.abs(lam))))
    log_a = LRU_C * r * log_sig
    a = jnp.exp(log_a)
    u = jnp.sqrt(-jnp.tanh(log_a) * (a * a + 1.0)) * (i * xc)

    groups = tl // HIST
    a3 = a.reshape(groups, HIST, D_LRU)
    u3 = u.reshape(groups, HIST, D_LRU)
    sub = lax.broadcasted_iota(jnp.int32, (groups, HIST, D_LRU), 1)
    d = 1
    while d < HIST:
        ok = sub >= d
        a_sh = pltpu.roll(a3, d, axis=1)
        u_sh = pltpu.roll(u3, d, axis=1)
        u3 = jnp.where(ok, a3 * u_sh + u3, u3)
        a3 = jnp.where(ok, a3 * a_sh, a3)
        d *= 2
    hb = h_ref[...]
    hs = []
    for k in range(groups):
        hk = a3[k] * hb + u3[k]
        hs.append(hk)
        hb = jnp.broadcast_to(hk[HIST - 1:HIST, :], (HIST, D_LRU))
    h_ref[...] = hb
    o_ref[0] = jnp.concatenate(hs, axis=0) * _gelu_tanh(zy_ref[0])


def _lru_call(zx, zy, conv_w, conv_b, w_ax, bias_ax, lam, l):
    nb, seq, _ = zx.shape
    tl = LRU_TILE
    row = pl.BlockSpec((1, tl, D_LRU), lambda b, t: (b, t, 0))
    return pl.pallas_call(
        _lru_kernel,
        grid=(nb, seq // tl),
        in_specs=[
            row, row,
            pl.BlockSpec((None, LRU_CONV_W, D_LRU), lambda b, t: (l, 0, 0)),
            pl.BlockSpec((None, 1, D_LRU), lambda b, t: (l, 0, 0)),
            pl.BlockSpec((None, D_LRU, 2 * D_LRU), lambda b, t: (l, 0, 0)),
            pl.BlockSpec((None, 2, D_LRU), lambda b, t: (l, 0, 0)),
            pl.BlockSpec((None, 1, D_LRU), lambda b, t: (l, 0, 0)),
        ],
        out_specs=row,
        out_shape=jax.ShapeDtypeStruct((nb, seq, D_LRU), F32),
        scratch_shapes=[pltpu.VMEM((HIST + tl, D_LRU), F32), pltpu.VMEM((HIST, D_LRU), F32)],
        compiler_params=pltpu.CompilerParams(dimension_semantics=("arbitrary", "arbitrary")),
        name="rglru",
    )(zx, zy, conv_w, conv_b, w_ax, bias_ax, lam)


def _ffn_kernel(x_ref, on_ref, ol_ref, mod_ref, gn_ref, gl_ref, wo_ref, g_ref, wg_ref, wu_ref, cw_ref, cb_ref,
                wd_ref, fg_ref, o_ref, hist_ref, act_ref, *, final_norm):
    t = pl.program_id(1)
    tm = x_ref.shape[1]

    @pl.when(t == 0)
    def _():
        hist_ref[...] = jnp.zeros(hist_ref.shape, F32)

    a = (_rms(on_ref[0]) * gn_ref[...]).astype(BF16)
    b = (_rms(ol_ref[0]) * gl_ref[...]).astype(BF16)
    x = x_ref[0] + mod_ref[2:3, :] * (_dot(a, wo_ref[:D_NSA, :]) + _dot(b, wo_ref[D_NSA:, :]))
    h = (_rms(x) * g_ref[...]) * (1.0 + mod_ref[4:5, :]) + mod_ref[3:4, :]
    hb = h.astype(BF16)
    for j in range(D_FF // FF_CHUNK):
        cs = slice(j * FF_CHUNK, (j + 1) * FF_CHUNK)
        gp = _dot(hb, wg_ref[:, cs])
        ge = jnp.concatenate([hist_ref[:, cs], gp], axis=0)
        hist_ref[:, cs] = gp[tm - HIST:, :]
        gate = cb_ref[:, cs] + gp * cw_ref[FFN_CONV_W - 1:FFN_CONV_W, cs]
        for k in range(FFN_CONV_W - 1):
            off = HIST - (FFN_CONV_W - 1) + k
            gate = gate + ge[off:off + tm, :] * cw_ref[k:k + 1, cs]
        act = (gate * _sigmoid(gate)) * _dot(hb, wu_ref[:, cs])
        act_ref[:, cs] = act.astype(BF16)
    y = x + mod_ref[5:6, :] * _dot(act_ref[...], wd_ref[...])
    if final_norm:
        y = _rms(y) * fg_ref[...]
    o_ref[0] = y


def _ffn_call(x, o_nsa, o_lru, mod, g_nsa, g_lru, w_out, norm_g, w_gate, w_up, conv_w, conv_b, w_down, final_g,
              l, final_norm):
    nb, seq, _ = x.shape
    tm = ROW_TILE
    row = pl.BlockSpec((1, tm, D_MODEL), lambda b, t: (b, t, 0))
    half_row = pl.BlockSpec((1, tm, D_NSA), lambda b, t: (b, t, 0))
    once = dict(pipeline_mode=pl.Buffered(1))
    return pl.pallas_call(
        functools.partial(_ffn_kernel, final_norm=final_norm),
        grid=(nb, seq // tm),
        in_specs=[
            row, half_row, half_row,
            pl.BlockSpec((None, None, 6, D_MODEL), lambda b, t: (l, b, 0, 0)),
            pl.BlockSpec((None, 1, D_NSA), lambda b, t: (l, 0, 0)),
            pl.BlockSpec((None, 1, D_LRU), lambda b, t: (l, 0, 0)),
            pl.BlockSpec((None, D_MODEL, D_MODEL), lambda b, t: (l, 0, 0), **once),
            pl.BlockSpec((None, 1, D_MODEL), lambda b, t: (l, 0, 0)),
            pl.BlockSpec((None, D_MODEL, D_FF), lambda b, t: (l, 0, 0), **once),
            pl.BlockSpec((None, D_MODEL, D_FF), lambda b, t: (l, 0, 0), **once),
            pl.BlockSpec((None, FFN_CONV_W, D_FF), lambda b, t: (l, 0, 0)),
            pl.BlockSpec((None, 1, D_FF), lambda b, t: (l, 0, 0)),
            pl.BlockSpec((None, D_FF, D_MODEL), lambda b, t: (l, 0, 0), **once),
            pl.BlockSpec((1, D_MODEL), lambda b, t: (0, 0)),
        ],
        out_specs=row,
        out_shape=jax.ShapeDtypeStruct(x.shape, F32),
        scratch_shapes=[pltpu.VMEM((HIST, D_FF), F32), pltpu.VMEM((tm, D_FF), BF16)],
        compiler_params=pltpu.CompilerParams(
            dimension_semantics=("arbitrary", "arbitrary"), vmem_limit_bytes=56 * 1024 * 1024),
        name="conv_ffn",
    )(x, o_nsa, o_lru, mod, g_nsa, g_lru, w_out, norm_g, w_gate, w_up, conv_w, conv_b, w_down, final_g)


def _in_perm():
    zero = IN_COLS
    q0, kc0, vc0 = 0, D_NSA, D_NSA + KV_W
    ks0, vs0, kw0, vw0 = (D_NSA + i * KV_W for i in range(2, 6))
    zg0 = D_NSA + 6 * KV_W
    zx0 = zg0 + N_BRANCH * HEADS
    zy0 = zx0 + D_LRU
    cols = list(range(q0, q0 + D_NSA)) + list(range(kc0, kc0 + KV_W)) + list(range(vc0, vc0 + KV_W))
    for a0, b0 in ((ks0, kw0), (vs0, vw0)):
        for g in range(KV_GROUPS):
            cols += list(range(a0 + g * HEAD_DIM, a0 + (g + 1) * HEAD_DIM))
            cols += list(range(b0 + g * HEAD_DIM, b0 + (g + 1) * HEAD_DIM))
    cols += list(range(zx0, zx0 + D_LRU)) + list(range(zy0, zy0 + D_LRU))
    per_g = N_BRANCH * REP
    for g in range(KV_GROUPS):
        cols += list(range(zg0 + g * per_g, zg0 + (g + 1) * per_g)) + [zero] * (LANES - per_g)
    assert len(cols) == IN_COLS_P
    return np.asarray(cols, np.int32)


def _prep_in(w_in, gate_b):
    perm = _in_perm()
    cuts = [0] + [i for i in range(1, len(perm)) if perm[i] != perm[i - 1] + 1 and not (perm[i] == perm[i - 1] == IN_COLS)]
    parts = []
    for a, b in zip(cuts, cuts[1:] + [len(perm)]):
        if perm[a] == IN_COLS:
            parts.append(jnp.zeros(w_in.shape[:-1] + (b - a,), BF16))
        else:
            parts.append(w_in[..., int(perm[a]):int(perm[a]) + (b - a)].astype(BF16))
    w_p = jnp.concatenate(parts, axis=-1)
    per_g = N_BRANCH * REP
    gb = gate_b.reshape(DEPTH, KV_GROUPS, per_g)
    gb = jnp.pad(gb, ((0, 0), (0, 0), (0, LANES - per_g))).reshape(DEPTH, 1, KV_GROUPS * LANES)
    return w_p, gb


def _prep_cmp(cmp_pos, cmp_w1, cmp_b1, cmp_w2, cmp_b2):
    half = CMP_LEN // 2
    pos = cmp_pos.reshape(DEPTH, 2, 2, half, 1, HEAD_DIM)
    pos = jnp.broadcast_to(pos, (DEPTH, 2, 2, half, KV_GROUPS, HEAD_DIM))
    pos_p = pos.reshape(DEPTH, 4, half * KV_W)
    w1 = cmp_w1.reshape(DEPTH, 2, 2, half, HEAD_DIM, CMP_HIDDEN)
    w1 = w1.astype(BF16)
    zero = jnp.zeros_like(w1)
    assert KV_GROUPS == 2
    w1_p = jnp.stack([jnp.concatenate([w1, zero], axis=-1), jnp.concatenate([zero, w1], axis=-1)], axis=4)
    w1_p = w1_p.reshape(DEPTH, 4, half * KV_W, KV_GROUPS * CMP_HIDDEN)
    b1_p = jnp.tile(cmp_b1, (1, 1, KV_GROUPS))
    zeros = jnp.zeros((DEPTH, CMP_HIDDEN, HEAD_DIM), cmp_w2.dtype)
    w2_p = jnp.concatenate([
        jnp.concatenate([cmp_w2[:, 0], zeros], axis=-1),
        jnp.concatenate([zeros, cmp_w2[:, 1]], axis=-1)], axis=1).astype(BF16)
    b2_p = cmp_b2.reshape(DEPTH, 1, 2 * HEAD_DIM)
    return pos_p, w1_p, b1_p, w2_p, b2_p


def _prep_lru(lru_wa, lru_wx, lru_ba, lru_bx):
    eye = jnp.eye(LRU_BLOCKS, dtype=lru_wa.dtype)
    dense = lambda w: jnp.einsum('lncd,nm->lncmd', w, eye).reshape(DEPTH, D_LRU, D_LRU)
    w_ax = jnp.concatenate([dense(lru_wa), dense(lru_wx)], axis=-1).astype(BF16)
    bias_ax = jnp.stack([lru_ba, lru_bx], axis=1)
    return w_ax, bias_ax


def _attn_consts(seq):
    nblk = seq // SEL_BLOCK
    ncb_p = seq // CMP_STRIDE
    cstart = np.arange(ncb_p) * CMP_STRIDE
    js = np.arange(nblk) * SEL_BLOCK
    overlap_t = ((cstart[None, :] < js[:, None] + SEL_BLOCK) & (cstart[None, :] + CMP_LEN > js[:, None]))
    return jnp.asarray(overlap_t.astype(np.float32), BF16)


def kernel(x, c, ada_w, ada_b, mix_norm_g, ffn_norm_g, w_in, nsa_gate_b, cmp_pos, cmp_w1, cmp_b1, cmp_w2, cmp_b2, lru_conv_w, lru_conv_b, lru_wa, lru_ba, lru_wx, lru_bx, lru_lambda, nsa_out_norm_g, lru_out_norm_g, w_out, ffn_w_gate, ffn_w_up, ffn_conv_w, ffn_conv_b, ffn_w_down, final_norm_g):
    nb, seq, _ = x.shape
    mod = _mod_call(c, ada_w, ada_b)
    w_in_p, gate_b_p = _prep_in(w_in, nsa_gate_b)
    pos_p, w1_p, b1_p, w2_p, b2_p = _prep_cmp(cmp_pos, cmp_w1, cmp_b1, cmp_w2, cmp_b2)
    w_ax, bias_ax = _prep_lru(lru_wa, lru_wx, lru_ba, lru_bx)
    overlap_t = _attn_consts(seq)
    w_out_b = w_out.astype(BF16)
    wg_b, wu_b, wd_b = ffn_w_gate.astype(BF16), ffn_w_up.astype(BF16), ffn_w_down.astype(BF16)
    row3 = lambda a: a.reshape(DEPTH, 1, a.shape[-1])
    mix_g, ffn_g = row3(mix_norm_g), row3(ffn_norm_g)
    nsa_g, lru_g = row3(nsa_out_norm_g), row3(lru_out_norm_g)
    lru_cb, lam, ffn_cb = row3(lru_conv_b), row3(lru_lambda), row3(ffn_conv_b)
    final_g = final_norm_g.reshape(1, D_MODEL)

    for l in range(DEPTH):
        q, kc, vc, kx, kk, vs, vw, zx, zy, gate = _in_call(x, mod, mix_g, w_in_p, gate_b_p, l)
        kvc = _cmp_call(kc, vc, pos_p, w1_p, b1_p, w2_p, b2_p, l)
        o_nsa = _attn_call(q, kx, kk, vs, vw, kvc, gate, overlap_t)
        o_lru = _lru_call(zx, zy, lru_conv_w, lru_cb, w_ax, bias_ax, lam, l)
        x = _ffn_call(x, o_nsa, o_lru, mod, nsa_g, lru_g, w_out_b, ffn_g, wg_b, wu_b, ffn_conv_w, ffn_cb, wd_b,
                      final_g, l, l == DEPTH - 1)
    return x
```

```python
import functools

import numpy as np
import jax
import jax.numpy as jnp
from jax import lax
from jax.experimental import pallas as pl
from jax.experimental.pallas import tpu as pltpu

F32 = jnp.float32
BF16 = jnp.bfloat16

D_MODEL = 1024
DEPTH = 4
D_NSA = 512
D_LRU = 512
HEADS = 8
HEAD_DIM = 64
KV_GROUPS = 2
REP = HEADS // KV_GROUPS
N_BRANCH = 3
CMP_LEN = 32
CMP_STRIDE = 16
CMP_HIDDEN = 4 * HEAD_DIM
SEL_BLOCK = 64
SEL_TOP_K = 8
WINDOW = 512
Q_BLOCK = 128
LRU_BLOCKS = 8
LRU_BLOCK_W = D_LRU // LRU_BLOCKS
LRU_CONV_W = 4
LRU_C = 8.0
D_FF = 2816
FFN_CONV_W = 3
NORM_EPS = 1e-6
NEG_INF = -1e30
FORCE_BONUS = 1e4

KV_W = KV_GROUPS * HEAD_DIM
IN_COLS = D_NSA + 6 * KV_W + N_BRANCH * HEADS + 2 * D_LRU
LANES = 128
KEY_CHUNK = 256
SEL_CHUNK = 512
ROW_TILE = 512
LRU_TILE = 1024
FF_CHUNK = 256
HIST = 8

C_Q = 0
C_KC = C_Q + D_NSA
C_VC = C_KC + KV_W
C_KK = C_VC + KV_W
C_VV = C_KK + KV_GROUPS * LANES
C_ZX = C_VV + KV_GROUPS * LANES
C_ZY = C_ZX + D_LRU
C_GATE = C_ZY + D_LRU
IN_COLS_P = C_GATE + KV_GROUPS * LANES


def _dot(a, b):
    return jnp.dot(a, b, preferred_element_type=F32)


def _dot_nt(a, b):
    return lax.dot_general(a, b, (((1,), (1,)), ((), ())), preferred_element_type=F32)


def _sigmoid(x):
    return 0.5 * jnp.tanh(0.5 * x) + 0.5


def _gelu_tanh(x):
    c = np.float32(np.sqrt(2.0 / np.pi))
    return x * (0.5 * (1.0 + jnp.tanh(c * (x + 0.044715 * (x * x * x)))))


def _rms(x):
    return x * lax.rsqrt(jnp.mean(x * x, axis=-1, keepdims=True) + NORM_EPS)


def _mod_kernel(c_ref, w_ref, b_ref, o_ref):
    c = c_ref[...]
    ca = (c * _sigmoid(c)).astype(BF16)
    o_ref[...] = _dot(ca, w_ref[...].astype(BF16)) + b_ref[...]


def _mod_call(c, ada_w, ada_b):
    nb = c.shape[0]
    out = pl.pallas_call(
        _mod_kernel,
        grid=(DEPTH, 6),
        in_specs=[
            pl.BlockSpec((nb, D_MODEL), lambda l, j: (0, 0)),
            pl.BlockSpec((None, D_MODEL, D_MODEL), lambda l, j: (l, 0, j)),
            pl.BlockSpec((None, 1, D_MODEL), lambda l, j: (l, 0, j)),
        ],
        out_specs=pl.BlockSpec((None, nb, D_MODEL), lambda l, j: (l, 0, j)),
        out_shape=jax.ShapeDtypeStruct((DEPTH, nb, 6 * D_MODEL), F32),
        name="adaln_mod",
    )(c, ada_w, ada_b.reshape(DEPTH, 1, 6 * D_MODEL))
    return out.reshape(DEPTH, nb, 6, D_MODEL)


def _in_kernel(x_ref, mod_ref, g_ref, w_ref, gb_ref,
               q_ref, kc_ref, vc_ref, kx_ref, kk_ref, vs_ref, vw_ref, zx_ref, zy_ref, gate_ref, tok_ref):
    x = x_ref[0]
    tm = x.shape[0]
    lane = lax.broadcasted_iota(jnp.int32, (tm, LANES), 1)
    tok = pl.program_id(1) * tm + lax.broadcasted_iota(jnp.int32, (tm, LANES), 0)
    blk_hot = jnp.where(lane - HEAD_DIM == jnp.right_shift(tok, int(np.log2(SEL_BLOCK))), 1.0, 0.0)
    h = (_rms(x) * g_ref[...]) * (1.0 + mod_ref[1:2, :]) + mod_ref[0:1, :]
    hb = h.astype(BF16)

    def seg(start, width):
        return _dot(hb, w_ref[:, start:start + width])

    q_ref[0] = (seg(C_Q, D_NSA) * (HEAD_DIM ** -0.5)).astype(BF16)
    for out_ref, start in ((kc_ref, C_KC), (vc_ref, C_VC)):
        tok_ref[...] = seg(start, KV_W)
        for i in range(CMP_STRIDE):
            out_ref[0, :, i * KV_W:(i + 1) * KV_W] = tok_ref[pl.ds(i, tm // CMP_STRIDE, stride=CMP_STRIDE), :]
    for g in range(KV_GROUPS):
        kk = seg(C_KK + g * LANES, LANES)
        kk_ref[0, g] = kk.astype(BF16)
        kx_ref[0, g] = jnp.where(lane < HEAD_DIM, kk, blk_hot).astype(BF16)
        vv = seg(C_VV + g * LANES, LANES)
        vs_ref[0, g] = jnp.where(lane < HEAD_DIM, vv, 1.0).astype(BF16)
        vw_ref[0, g] = jnp.where(lane < HEAD_DIM, 1.0, vv).astype(BF16)
        gate_ref[0, g] = _sigmoid(seg(C_GATE + g * LANES, LANES) + gb_ref[:, g * LANES:(g + 1) * LANES])
    zx_ref[0] = seg(C_ZX, D_LRU)
    zy_ref[0] = seg(C_ZY, D_LRU)


def _in_call(x, mod, norm_g, w_in_p, gate_b_p, l):
    nb, seq, _ = x.shape
    tm = ROW_TILE
    row = lambda width: pl.BlockSpec((1, tm, width), lambda b, t: (b, t, 0))
    grp = pl.BlockSpec((1, KV_GROUPS, tm, LANES), lambda b, t: (b, 0, t, 0))
    cmp_rows, cmp_width = tm // CMP_STRIDE, CMP_STRIDE * KV_W
    cmp_in = pl.BlockSpec((1, cmp_rows, cmp_width), lambda b, t: (b, t, 0))
    return pl.pallas_call(
        _in_kernel,
        grid=(nb, seq // tm),
        in_specs=[
            row(D_MODEL),
            pl.BlockSpec((None, None, 6, D_MODEL), lambda b, t: (l, b, 0, 0)),
            pl.BlockSpec((None, 1, D_MODEL), lambda b, t: (l, 0, 0)),
            pl.BlockSpec((None, D_MODEL, IN_COLS_P), lambda b, t: (l, 0, 0)),
            pl.BlockSpec((None, 1, KV_GROUPS * LANES), lambda b, t: (l, 0, 0)),
        ],
        out_specs=[row(D_NSA), cmp_in, cmp_in, grp, grp, grp, grp, row(D_LRU), row(D_LRU), grp],
        out_shape=[
            jax.ShapeDtypeStruct((nb, seq, D_NSA), BF16),
            jax.ShapeDtypeStruct((nb, seq // CMP_STRIDE, cmp_width), F32),
            jax.ShapeDtypeStruct((nb, seq // CMP_STRIDE, cmp_width), F32),
            jax.ShapeDtypeStruct((nb, KV_GROUPS, seq, LANES), BF16),
            jax.ShapeDtypeStruct((nb, KV_GROUPS, seq, LANES), BF16),
            jax.ShapeDtypeStruct((nb, KV_GROUPS, seq, LANES), BF16),
            jax.ShapeDtypeStruct((nb, KV_GROUPS, seq, LANES), BF16),
            jax.ShapeDtypeStruct((nb, seq, D_LRU), F32),
            jax.ShapeDtypeStruct((nb, seq, D_LRU), F32),
            jax.ShapeDtypeStruct((nb, KV_GROUPS, seq, LANES), F32),
        ],
        scratch_shapes=[pltpu.VMEM((tm, KV_W), F32)],
        compiler_params=pltpu.CompilerParams(
            dimension_semantics=("arbitrary", "arbitrary"), vmem_limit_bytes=48 * 1024 * 1024),
        name="mixer_in",
    )(x, mod, norm_g, w_in_p, gate_b_p)


def _cmp_kernel(kc_ref, vc_ref, pos_ref, w1_ref, b1_ref, w2_ref, b2_ref, o_ref):
    def hidden(x_ref, i):
        x = x_ref[0]
        top = (x + pos_ref[2 * i:2 * i + 1, :]).astype(BF16)
        bot = (x + pos_ref[2 * i + 1:2 * i + 2, :]).astype(BF16)
        a = _dot(top, w1_ref[2 * i])
        b = _dot(bot, w1_ref[2 * i + 1])
        nrow = b.shape[0]
        b = pltpu.roll(b, nrow - 1, axis=0)
        return _gelu_tanh(a + b + b1_ref[i:i + 1, :])

    hk = hidden(kc_ref, 0)
    hv = hidden(vc_ref, 1)
    for g in range(KV_GROUPS):
        sl = slice(g * CMP_HIDDEN, (g + 1) * CMP_HIDDEN)
        hg = jnp.concatenate([hk[:, sl], hv[:, sl]], axis=1).astype(BF16)
        o_ref[0, g] = (_dot(hg, w2_ref[...]) + b2_ref[...]).astype(BF16)


def _cmp_call(kc16, vc16, pos_p, w1_p, b1_p, w2_p, b2_p, l):
    nb, nrow, width = kc16.shape
    blk = pl.BlockSpec((1, nrow, width), lambda b: (b, 0, 0))
    return pl.pallas_call(
        _cmp_kernel,
        grid=(nb,),
        in_specs=[
            blk, blk,
            pl.BlockSpec((None, 4, width), lambda b: (l, 0, 0)),
            pl.BlockSpec((None, 4, width, KV_GROUPS * CMP_HIDDEN), lambda b: (l, 0, 0, 0)),
            pl.BlockSpec((None, 2, KV_GROUPS * CMP_HIDDEN), lambda b: (l, 0, 0)),
            pl.BlockSpec((None, 2 * CMP_HIDDEN, LANES), lambda b: (l, 0, 0)),
            pl.BlockSpec((None, 1, LANES), lambda b: (l, 0, 0)),
        ],
        out_specs=pl.BlockSpec((1, KV_GROUPS, nrow, LANES), lambda b: (b, 0, 0, 0)),
        out_shape=jax.ShapeDtypeStruct((nb, KV_GROUPS, nrow, LANES), BF16),
        compiler_params=pltpu.CompilerParams(
            dimension_semantics=("arbitrary",), vmem_limit_bytes=48 * 1024 * 1024),
        name="nsa_compress",
    )(kc16, vc16, pos_p, w1_p, b1_p, w2_p, b2_p)


def _attn_body(n_sel, q_ref, kx_ref, kk_ref, vs_ref, vw_ref, kvc_ref, gate_ref, ot_ref, o_ref):
    qb = pl.program_id(2)
    nq = Q_BLOCK
    nrow = REP * nq
    q = q_ref[0]
    qs = jnp.concatenate([q[:, r * HEAD_DIM:(r + 1) * HEAD_DIM] for r in range(REP)], axis=0)
    zpad = jnp.zeros_like(qs)
    q_lo = jnp.concatenate([qs, zpad], axis=1)
    q_hi = jnp.concatenate([zpad, qs], axis=1)

    def tok_rows(width):
        return qb * nq + lax.broadcasted_iota(jnp.int32, (nq, width), 0)

    lane_row = lax.broadcasted_iota(jnp.int32, (nrow, LANES), 1)

    def lane_max(x):
        return functools.reduce(jnp.maximum, [x[:, i:i + LANES] for i in range(0, x.shape[1], LANES)])

    def rep_lanes(x, width):
        return jnp.concatenate([x] * (width // LANES), axis=1)

    def row_bcast(x):
        return jnp.broadcast_to(x, (x.shape[0], LANES))

    gt = gate_ref[0, 0]

    def gate_rows(br):
        return jnp.concatenate(
            [jnp.broadcast_to(gt[:, N_BRANCH * r + br:N_BRANCH * r + br + 1], (nq, LANES)) for r in range(REP)],
            axis=0)

    kvc = kvc_ref[0, 0]
    ncb = kvc.shape[0]
    sc = _dot_nt(q_lo, kvc).reshape(REP, nq, ncb)
    cidx = lax.broadcasted_iota(jnp.int32, (nq, ncb), 1)
    mask_c = (cidx * CMP_STRIDE + (CMP_LEN - 1)) <= tok_rows(ncb)
    sc = jnp.where(mask_c[None], sc, NEG_INF)
    mc = jnp.max(sc, axis=-1, keepdims=True)
    pc = jnp.where(mask_c[None], jnp.exp(sc - mc), 0.0)
    lc = jnp.sum(pc, axis=-1, keepdims=True)
    pc = pc / jnp.where(lc > 0.0, lc, 1.0)
    o_cmp = _dot(pc.reshape(nrow, ncb).astype(BF16), kvc)

    psum = pc[0] + pc[1] + pc[2] + pc[3]
    p_hi = psum.astype(BF16)
    p_lo = (psum - p_hi.astype(F32)).astype(BF16)
    ot = ot_ref[...]
    imp = _dot_nt(ot, p_hi) + _dot_nt(ot, p_lo)
    nblk = imp.shape[0]
    jj = lax.broadcasted_iota(jnp.int32, (nblk, nq), 0)
    lane = lax.broadcasted_iota(jnp.int32, (nblk, nq), 1)
    qblk = jnp.right_shift(qb * nq + lane, int(np.log2(SEL_BLOCK)))
    valid = jj <= qblk
    forced = (jj == 0) | (jj == qblk) | (jj == qblk - 1)
    imp = jnp.where(valid, imp + jnp.where(forced, FORCE_BONUS, 0.0), NEG_INF)
    rank = jnp.zeros((nblk, nq), F32)
    for i in range(nblk):
        row = imp[i:i + 1, :]
        beats = (row > imp) | ((row == imp) & (jj > i))
        rank = rank + jnp.where(beats, 1.0, 0.0)
    sel_t = jnp.where(rank < float(SEL_TOP_K), 1.0, 0.0)
    sel_t = jnp.concatenate(
        [sel_t, jnp.ones((HEAD_DIM - nblk, nq), F32), jnp.zeros((LANES - HEAD_DIM, nq), F32)], axis=0)
    selb = ((sel_t.T[:, :HEAD_DIM] - 1.0) * -NEG_INF).astype(BF16)
    q_sel = jnp.concatenate([qs, jnp.concatenate([selb] * REP, axis=0)], axis=1)

    n_win = WINDOW // KEY_CHUNK + (0 if n_sel == 1 else 1)
    diag = qb // (KEY_CHUNK // nq)
    k0_win, ok_win = [], []
    for i in range(n_win):
        cw = diag - (n_win - 1) + i
        k0_win.append(pl.multiple_of(jnp.maximum(cw, 0) * KEY_CHUNK, KEY_CHUNK))
        kpos = cw * KEY_CHUNK + lax.broadcasted_iota(jnp.int32, (nq, KEY_CHUNK), 1)
        tq = tok_rows(KEY_CHUNK)
        ok_win.append((kpos <= tq) & (kpos > tq - WINDOW) & (kpos >= 0))
    half = nrow // 2
    acc_halves = []
    for hh in range(2):
        qh = q_hi[hh * half:(hh + 1) * half]
        s_win = []
        for k0, ok in zip(k0_win, ok_win):
            s = _dot_nt(qh, kk_ref[0, 0, pl.ds(k0, KEY_CHUNK), :]).reshape(REP // 2, nq, KEY_CHUNK)
            s_win.append(jnp.where(ok[None], s, NEG_INF).reshape(half, KEY_CHUNK))
        m_w = row_bcast(jnp.max(functools.reduce(jnp.maximum, [lane_max(s) for s in s_win]), axis=1, keepdims=True))
        acc_halves.append(functools.reduce(jnp.add, [
            _dot(jnp.exp(s - rep_lanes(m_w, KEY_CHUNK)).astype(BF16), vw_ref[0, 0, pl.ds(k0, KEY_CHUNK), :])
            for s, k0 in zip(s_win, k0_win)]))
    acc_w = jnp.concatenate(acc_halves, axis=0)
    o_hi = gate_rows(2) * acc_w / pltpu.roll(acc_w, HEAD_DIM, axis=1) + gate_rows(0) * o_cmp

    sel_parts = 1 if n_sel == 1 else 2
    part = nrow // sel_parts
    acc_parts = []
    for hh in range(sel_parts):
        qh = q_sel[hh * part:(hh + 1) * part]
        s_sel = []
        for c in range(n_sel):
            s = _dot_nt(qh, kx_ref[0, 0, c * SEL_CHUNK:(c + 1) * SEL_CHUNK, :])
            if c == n_sel - 1:
                kpos = c * SEL_CHUNK + lax.broadcasted_iota(jnp.int32, (nq, SEL_CHUNK), 1)
                ok = kpos <= tok_rows(SEL_CHUNK)
                s = jnp.where(ok[None], s.reshape(part // nq, nq, SEL_CHUNK), NEG_INF).reshape(part, SEL_CHUNK)
            s_sel.append(s)
        m_s = row_bcast(jnp.max(functools.reduce(jnp.maximum, [lane_max(s) for s in s_sel]), axis=1, keepdims=True))
        acc_parts.append(functools.reduce(jnp.add, [
            _dot(jnp.exp(s - rep_lanes(m_s, SEL_CHUNK)).astype(BF16),
                 vs_ref[0, 0, c * SEL_CHUNK:(c + 1) * SEL_CHUNK, :])
            for c, s in enumerate(s_sel)]))
    acc_s = jnp.concatenate(acc_parts, axis=0)
    o_lo = gate_rows(1) * acc_s / pltpu.roll(acc_s, HEAD_DIM, axis=1)

    t = jnp.where(lane_row < HEAD_DIM, o_lo, o_hi)
    u = t + pltpu.roll(t, HEAD_DIM, axis=1)
    lane_q = lax.broadcasted_iota(jnp.int32, (nq, LANES), 1)
    o_ref[0] = jnp.concatenate(
        [jnp.where(lane_q < HEAD_DIM, u[(2 * i) * nq:(2 * i + 1) * nq], u[(2 * i + 1) * nq:(2 * i + 2) * nq])
         for i in range(REP // 2)], axis=1)


def _attn_kernel(*refs):
    chunks = pl.program_id(2) // (SEL_CHUNK // Q_BLOCK) + 1
    for n_sel in range(1, refs[1].shape[2] // SEL_CHUNK + 1):
        pl.when(chunks == n_sel)(functools.partial(_attn_body, n_sel, *refs))


def _attn_call(q, kx, kk, vs, vw, kvc, gate, overlap_t):
    nb, seq, _ = q.shape
    nq = seq // Q_BLOCK
    ncb = kvc.shape[2]
    per_bg = lambda rows: pl.BlockSpec((1, 1, rows, LANES), lambda b, g, i: (b, g, 0, 0))
    return pl.pallas_call(
        _attn_kernel,
        grid=(nb, KV_GROUPS, nq),
        in_specs=[
            pl.BlockSpec((1, Q_BLOCK, REP * HEAD_DIM), lambda b, g, i: (b, i, g)),
            per_bg(seq), per_bg(seq), per_bg(seq), per_bg(seq), per_bg(ncb),
            pl.BlockSpec((1, 1, Q_BLOCK, LANES), lambda b, g, i: (b, g, i, 0)),
            pl.BlockSpec(overlap_t.shape, lambda b, g, i: (0, 0)),
        ],
        out_specs=pl.BlockSpec((1, Q_BLOCK, REP * HEAD_DIM), lambda b, g, i: (b, i, g)),
        out_shape=jax.ShapeDtypeStruct((nb, seq, D_NSA), F32),
        compiler_params=pltpu.CompilerParams(
            dimension_semantics=("arbitrary", "arbitrary", "arbitrary"),
            vmem_limit_bytes=48 * 1024 * 1024),
        name="nsa_attention",
    )(q, kx, kk, vs, vw, kvc, gate, overlap_t)


def _lru_kernel(zx_ref, zy_ref, cw_ref, cb_ref, w_ref, bias_ref, lam_ref, o_ref, hist_ref, h_ref):
    t = pl.program_id(2)
    tl = zx_ref.shape[1]
    ng = tl // HIST
    taps = LRU_CONV_W - 1

    @pl.when(t == 0)
    def _():
        hist_ref[...] = jnp.zeros(hist_ref.shape, F32)
        h_ref[...] = jnp.zeros(h_ref.shape, F32)

    xs = [zx_ref[0, pl.ds(j, HIST, stride=ng), :] for j in range(ng)]
    sub = lax.broadcasted_iota(jnp.int32, (HIST, LANES), 0)
    xe = [jnp.where(sub == 0, hist_ref[k:k + 1, :], pltpu.roll(xs[ng - taps + k], 1, axis=0))
          for k in range(taps)] + xs
    for k in range(taps):
        hist_ref[k:k + 1, :] = xs[ng - taps + k][HIST - 1:HIST, :]
    cw = [jnp.broadcast_to(cw_ref[k:k + 1, :], (HIST, LANES)) for k in range(LRU_CONV_W)]
    cb = jnp.broadcast_to(cb_ref[...], (HIST, LANES))
    xc = jnp.concatenate(
        [cb + functools.reduce(jnp.add, [cw[k] * xe[j + k] for k in range(LRU_CONV_W)]) for j in range(ng)], axis=0)

    y = _dot(xc.astype(BF16), w_ref[...])
    r = _sigmoid(y[:, :LANES] + bias_ref[0:1, :])
    i = _sigmoid(y[:, LANES:] + bias_ref[1:2, :])
    lam = lam_ref[...]
    log_sig = -(jnp.maximum(-lam, 0.0) + jnp.log1p(jnp.exp(-jnp.abs(lam))))
    log_a = LRU_C * r * log_sig
    a = jnp.exp(log_a)
    u = jnp.sqrt(-jnp.tanh(log_a) * (a * a + 1.0)) * (i * xc)

    hs, ps = [u[:HIST]], [a[:HIST]]
    for j in range(1, ng):
        aj = a[j * HIST:(j + 1) * HIST]
        hs.append(aj * hs[-1] + u[j * HIST:(j + 1) * HIST])
        ps.append(aj * ps[-1])
    c = h_ref[0:1, :]
    carry = [c]
    for rr in range(HIST - 1):
        c = ps[-1][rr:rr + 1, :] * c + hs[-1][rr:rr + 1, :]
        carry.append(c)
    h_ref[0:1, :] = ps[-1][HIST - 1:HIST, :] * c + hs[-1][HIST - 1:HIST, :]
    carry = jnp.concatenate(carry, axis=0)
    for j in range(ng):
        h = hs[j] + ps[j] * carry
        o_ref[0, pl.ds(j, HIST, stride=ng), :] = h * _gelu_tanh(zy_ref[0, pl.ds(j, HIST, stride=ng), :])


def _lru_call(zx, zy, conv_w, conv_b, w_ax, bias_ax, lam, l):
    nb, seq, _ = zx.shape
    tl = LRU_TILE
    row = pl.BlockSpec((1, tl, LANES), lambda b, c, t: (b, t, c))
    par = lambda rows: pl.BlockSpec((None, rows, LANES), lambda b, c, t: (l, 0, c))
    return pl.pallas_call(
        _lru_kernel,
        grid=(nb, D_LRU // LANES, seq // tl),
        in_specs=[
            row, row, par(LRU_CONV_W), par(1),
            pl.BlockSpec((None, None, LANES, 2 * LANES), lambda b, c, t: (l, c, 0, 0)),
            par(2), par(1),
        ],
        out_specs=row,
        out_shape=jax.ShapeDtypeStruct((nb, seq, D_LRU), F32),
        scratch_shapes=[pltpu.VMEM((HIST, LANES), F32), pltpu.VMEM((HIST, LANES), F32)],
        compiler_params=pltpu.CompilerParams(dimension_semantics=("arbitrary", "arbitrary", "arbitrary")),
        name="rglru",
    )(zx, zy, conv_w, conv_b, w_ax, bias_ax, lam)


def _ffn_kernel(x_ref, on_ref, ol_ref, mod_ref, gn_ref, gl_ref, wo_ref, g_ref, wg_ref, wu_ref, cw_ref, cb_ref,
                wd_ref, fg_ref, o_ref, hist_ref, act_ref, *, final_norm):
    t = pl.program_id(1)
    tm = x_ref.shape[1]

    @pl.when(t == 0)
    def _():
        hist_ref[...] = jnp.zeros(hist_ref.shape, F32)

    a = (_rms(on_ref[0]) * gn_ref[...]).astype(BF16)
    b = (_rms(ol_ref[0]) * gl_ref[...]).astype(BF16)
    x = x_ref[0] + mod_ref[2:3, :] * (_dot(a, wo_ref[:D_NSA, :]) + _dot(b, wo_ref[D_NSA:, :]))
    h = (_rms(x) * g_ref[...]) * (1.0 + mod_ref[4:5, :]) + mod_ref[3:4, :]
    hb = h.astype(BF16)
    for j in range(D_FF // FF_CHUNK):
        cs = slice(j * FF_CHUNK, (j + 1) * FF_CHUNK)
        gp = _dot(hb, wg_ref[:, cs])
        ge = jnp.concatenate([hist_ref[:, cs], gp], axis=0)
        hist_ref[:, cs] = gp[tm - HIST:, :]
        gate = cb_ref[:, cs] + gp * cw_ref[FFN_CONV_W - 1:FFN_CONV_W, cs]
        for k in range(FFN_CONV_W - 1):
            off = HIST - (FFN_CONV_W - 1) + k
            gate = gate + ge[off:off + tm, :] * cw_ref[k:k + 1, cs]
        act = (gate * _sigmoid(gate)) * _dot(hb, wu_ref[:, cs])
        act_ref[:, cs] = act.astype(BF16)
    y = x + mod_ref[5:6, :] * _dot(act_ref[...], wd_ref[...])
    if final_norm:
        y = _rms(y) * fg_ref[...]
    o_ref[0] = y


def _ffn_call(x, o_nsa, o_lru, mod, g_nsa, g_lru, w_out, norm_g, w_gate, w_up, conv_w, conv_b, w_down, final_g,
              l, final_norm):
    nb, seq, _ = x.shape
    tm = ROW_TILE
    row = pl.BlockSpec((1, tm, D_MODEL), lambda b, t: (b, t, 0))
    half_row = pl.BlockSpec((1, tm, D_NSA), lambda b, t: (b, t, 0))
    once = dict(pipeline_mode=pl.Buffered(1))
    return pl.pallas_call(
        functools.partial(_ffn_kernel, final_norm=final_norm),
        grid=(nb, seq // tm),
        in_specs=[
            row, half_row, half_row,
            pl.BlockSpec((None, None, 6, D_MODEL), lambda b, t: (l, b, 0, 0)),
            pl.BlockSpec((None, 1, D_NSA), lambda b, t: (l, 0, 0)),
            pl.BlockSpec((None, 1, D_LRU), lambda b, t: (l, 0, 0)),
            pl.BlockSpec((None, D_MODEL, D_MODEL), lambda b, t: (l, 0, 0), **once),
            pl.BlockSpec((None, 1, D_MODEL), lambda b, t: (l, 0, 0)),
            pl.BlockSpec((None, D_MODEL, D_FF), lambda b, t: (l, 0, 0), **once),
            pl.BlockSpec((None, D_MODEL, D_FF), lambda b, t: (l, 0, 0), **once),
            pl.BlockSpec((None, FFN_CONV_W, D_FF), lambda b, t: (l, 0, 0)),
            pl.BlockSpec((None, 1, D_FF), lambda b, t: (l, 0, 0)),
            pl.BlockSpec((None, D_FF, D_MODEL), lambda b, t: (l, 0, 0), **once),
            pl.BlockSpec((1, D_MODEL), lambda b, t: (0, 0)),
        ],
        out_specs=row,
        out_shape=jax.ShapeDtypeStruct(x.shape, F32),
        scratch_shapes=[pltpu.VMEM((HIST, D_FF), F32), pltpu.VMEM((tm, D_FF), BF16)],
        compiler_params=pltpu.CompilerParams(
            dimension_semantics=("arbitrary", "arbitrary"), vmem_limit_bytes=56 * 1024 * 1024),
        name="conv_ffn",
    )(x, o_nsa, o_lru, mod, g_nsa, g_lru, w_out, norm_g, w_gate, w_up, conv_w, conv_b, w_down, final_g)


def _in_perm():
    zero = IN_COLS
    q0, kc0, vc0 = 0, D_NSA, D_NSA + KV_W
    ks0, vs0, kw0, vw0 = (D_NSA + i * KV_W for i in range(2, 6))
    zg0 = D_NSA + 6 * KV_W
    zx0 = zg0 + N_BRANCH * HEADS
    zy0 = zx0 + D_LRU
    cols = list(range(q0, q0 + D_NSA)) + list(range(kc0, kc0 + KV_W)) + list(range(vc0, vc0 + KV_W))
    for a0, b0 in ((ks0, kw0), (vs0, vw0)):
        for g in range(KV_GROUPS):
            cols += list(range(a0 + g * HEAD_DIM, a0 + (g + 1) * HEAD_DIM))
            cols += list(range(b0 + g * HEAD_DIM, b0 + (g + 1) * HEAD_DIM))
    cols += list(range(zx0, zx0 + D_LRU)) + list(range(zy0, zy0 + D_LRU))
    per_g = N_BRANCH * REP
    for g in range(KV_GROUPS):
        cols += list(range(zg0 + g * per_g, zg0 + (g + 1) * per_g)) + [zero] * (LANES - per_g)
    assert len(cols) == IN_COLS_P
    return np.asarray(cols, np.int32)


def _prep_in(w_in, gate_b):
    perm = _in_perm()
    cuts = [0] + [i for i in range(1, len(perm)) if perm[i] != perm[i - 1] + 1 and not (perm[i] == perm[i - 1] == IN_COLS)]
    parts = []
    for a, b in zip(cuts, cuts[1:] + [len(perm)]):
        if perm[a] == IN_COLS:
            parts.append(jnp.zeros(w_in.shape[:-1] + (b - a,), BF16))
        else:
            parts.append(w_in[..., int(perm[a]):int(perm[a]) + (b - a)].astype(BF16))
    w_p = jnp.concatenate(parts, axis=-1)
    per_g = N_BRANCH * REP
    gb = gate_b.reshape(DEPTH, KV_GROUPS, per_g)
    gb = jnp.pad(gb, ((0, 0), (0, 0), (0, LANES - per_g))).reshape(DEPTH, 1, KV_GROUPS * LANES)
    return w_p, gb


def _prep_cmp(cmp_pos, cmp_w1, cmp_b1, cmp_w2, cmp_b2):
    half = CMP_LEN // 2
    pos = cmp_pos.reshape(DEPTH, 2, 2, half, 1, HEAD_DIM)
    pos = jnp.broadcast_to(pos, (DEPTH, 2, 2, half, KV_GROUPS, HEAD_DIM))
    pos_p = pos.reshape(DEPTH, 4, half * KV_W)
    w1 = cmp_w1.reshape(DEPTH, 2, 2, half, HEAD_DIM, CMP_HIDDEN)
    w1 = w1.astype(BF16)
    zero = jnp.zeros_like(w1)
    assert KV_GROUPS == 2
    w1_p = jnp.stack([jnp.concatenate([w1, zero], axis=-1), jnp.concatenate([zero, w1], axis=-1)], axis=4)
    w1_p = w1_p.reshape(DEPTH, 4, half * KV_W, KV_GROUPS * CMP_HIDDEN)
    b1_p = jnp.tile(cmp_b1, (1, 1, KV_GROUPS))
    zeros = jnp.zeros((DEPTH, CMP_HIDDEN, HEAD_DIM), cmp_w2.dtype)
    w2_p = jnp.concatenate([
        jnp.concatenate([cmp_w2[:, 0], zeros], axis=-1),
        jnp.concatenate([zeros, cmp_w2[:, 1]], axis=-1)], axis=1).astype(BF16)
    b2_p = cmp_b2.reshape(DEPTH, 1, 2 * HEAD_DIM)
    return pos_p, w1_p, b1_p, w2_p, b2_p


def _prep_lru(lru_wa, lru_wx, lru_ba, lru_bx):
    per_tile = LANES // LRU_BLOCK_W
    assert per_tile == 2

    def tiles(w):
        w = w.reshape(DEPTH, LRU_BLOCKS // per_tile, per_tile, LRU_BLOCK_W, LRU_BLOCK_W)
        zero = jnp.zeros_like(w[:, :, 0])
        return jnp.concatenate([jnp.concatenate([w[:, :, 0], zero], axis=-1),
                                jnp.concatenate([zero, w[:, :, 1]], axis=-1)], axis=-2)

    w_ax = jnp.concatenate([tiles(lru_wa), tiles(lru_wx)], axis=-1).astype(BF16)
    bias_ax = jnp.stack([lru_ba, lru_bx], axis=1)
    return w_ax, bias_ax


def _attn_consts(seq):
    nblk = seq // SEL_BLOCK
    ncb_p = seq // CMP_STRIDE
    cstart = np.arange(ncb_p) * CMP_STRIDE
    js = np.arange(nblk) * SEL_BLOCK
    overlap_t = ((cstart[None, :] < js[:, None] + SEL_BLOCK) & (cstart[None, :] + CMP_LEN > js[:, None]))
    return jnp.asarray(overlap_t.astype(np.float32), BF16)


def kernel(x, c, ada_w, ada_b, mix_norm_g, ffn_norm_g, w_in, nsa_gate_b, cmp_pos, cmp_w1, cmp_b1, cmp_w2, cmp_b2, lru_conv_w, lru_conv_b, lru_wa, lru_ba, lru_wx, lru_bx, lru_lambda, nsa_out_norm_g, lru_out_norm_g, w_out, ffn_w_gate, ffn_w_up, ffn_conv_w, ffn_conv_b, ffn_w_down, final_norm_g):
    nb, seq, _ = x.shape
    mod = _mod_call(c, ada_w, ada_b)
    w_in_p, gate_b_p = _prep_in(w_in, nsa_gate_b)
    pos_p, w1_p, b1_p, w2_p, b2_p = _prep_cmp(cmp_pos, cmp_w1, cmp_b1, cmp_w2, cmp_b2)
    w_ax, bias_ax = _prep_lru(lru_wa, lru_wx, lru_ba, lru_bx)
    overlap_t = _attn_consts(seq)
    w_out_b = w_out.astype(BF16)
    wg_b, wu_b, wd_b = ffn_w_gate.astype(BF16), ffn_w_up.astype(BF16), ffn_w_down.astype(BF16)
    row3 = lambda a: a.reshape(DEPTH, 1, a.shape[-1])
    mix_g, ffn_g = row3(mix_norm_g), row3(ffn_norm_g)
    nsa_g, lru_g = row3(nsa_out_norm_g), row3(lru_out_norm_g)
    lru_cb, lam, ffn_cb = row3(lru_conv_b), row3(lru_lambda), row3(ffn_conv_b)
    final_g = final_norm_g.reshape(1, D_MODEL)

    for l in range(DEPTH):
        q, kc, vc, kx, kk, vs, vw, zx, zy, gate = _in_call(x, mod, mix_g, w_in_p, gate_b_p, l)
        kvc = _cmp_call(kc, vc, pos_p, w1_p, b1_p, w2_p, b2_p, l)
        o_nsa = _attn_call(q, kx, kk, vs, vw, kvc, gate, overlap_t)
        o_lru = _lru_call(zx, zy, lru_conv_w, lru_cb, w_ax, bias_ax, lam, l)
        x = _ffn_call(x, o_nsa, o_lru, mod, nsa_g, lru_g, w_out_b, ffn_g, wg_b, wu_b, ffn_conv_w, ffn_cb, wd_b,
                      final_g, l, l == DEPTH - 1)
    return x
```

```python
import functools

import numpy as np
import jax
import jax.numpy as jnp
from jax import lax
from jax.experimental import pallas as pl
from jax.experimental.pallas import tpu as pltpu

F32 = jnp.float32
BF16 = jnp.bfloat16

D_MODEL = 1024
DEPTH = 4
D_NSA = 512
D_LRU = 512
HEADS = 8
HEAD_DIM = 64
KV_GROUPS = 2
REP = HEADS // KV_GROUPS
N_BRANCH = 3
CMP_LEN = 32
CMP_STRIDE = 16
CMP_HIDDEN = 4 * HEAD_DIM
SEL_BLOCK = 64
SEL_TOP_K = 8
WINDOW = 512
Q_BLOCK = 128
LRU_BLOCKS = 8
LRU_BLOCK_W = D_LRU // LRU_BLOCKS
LRU_CONV_W = 4
LRU_C = 8.0
D_FF = 2816
FFN_CONV_W = 3
NORM_EPS = 1e-6
NEG_INF = -1e30
FORCE_BONUS = 1e4

KV_W = KV_GROUPS * HEAD_DIM
IN_COLS = D_NSA + 6 * KV_W + N_BRANCH * HEADS + 2 * D_LRU
LANES = 128
KEY_CHUNK = 256
SEL_CHUNK = 512
QB_PER_STEP = 2
assert SEL_CHUNK % (QB_PER_STEP * Q_BLOCK) == 0
ROW_TILE = 512
LRU_TILE = 1024
FF_CHUNK = 256
HIST = 8

C_Q = 0
C_KC = C_Q + D_NSA
C_VC = C_KC + KV_W
C_KS, C_VS, C_KW, C_VW = (C_VC + (i + 1) * KV_W for i in range(4))
C_ZX = C_VW + KV_W
C_ZY = C_ZX + D_LRU
C_GATE = C_ZY + D_LRU
IN_COLS_P = C_GATE + LANES
GATES_PER_GROUP = N_BRANCH * REP


def _dot(a, b):
    return jnp.dot(a, b, preferred_element_type=F32)


def _dot_nt(a, b):
    return lax.dot_general(a, b, (((1,), (1,)), ((), ())), preferred_element_type=F32)


def _sigmoid(x):
    return 0.5 * jnp.tanh(0.5 * x) + 0.5


def _gelu_tanh(x):
    c = np.float32(np.sqrt(2.0 / np.pi))
    return x * (0.5 * (1.0 + jnp.tanh(c * (x + 0.044715 * (x * x * x)))))


def _rms(x):
    return x * lax.rsqrt(jnp.mean(x * x, axis=-1, keepdims=True) + NORM_EPS)


def _mod_kernel(c_ref, w_ref, b_ref, o_ref):
    c = c_ref[...]
    ca = (c * _sigmoid(c)).astype(BF16)
    o_ref[...] = _dot(ca, w_ref[...].astype(BF16)) + b_ref[...]


def _mod_call(c, ada_w, ada_b):
    nb = c.shape[0]
    out = pl.pallas_call(
        _mod_kernel,
        grid=(DEPTH, 6),
        in_specs=[
            pl.BlockSpec((nb, D_MODEL), lambda l, j: (0, 0)),
            pl.BlockSpec((None, D_MODEL, D_MODEL), lambda l, j: (l, 0, j)),
            pl.BlockSpec((None, 1, D_MODEL), lambda l, j: (l, 0, j)),
        ],
        out_specs=pl.BlockSpec((None, nb, D_MODEL), lambda l, j: (l, 0, j)),
        out_shape=jax.ShapeDtypeStruct((DEPTH, nb, 6 * D_MODEL), F32),
        name="adaln_mod",
    )(c, ada_w, ada_b.reshape(DEPTH, 1, 6 * D_MODEL))
    return out.reshape(DEPTH, nb, 6, D_MODEL)


def _in_kernel(x_ref, mod_ref, g_ref, w_ref, gb_ref,
               q_ref, kc_ref, vc_ref, kx_ref, kk_ref, vs_ref, vw_ref, zx_ref, zy_ref, gate_ref, tok_ref):
    x = x_ref[0]
    tm = x.shape[0]
    lane = lax.broadcasted_iota(jnp.int32, (tm, LANES), 1)
    tok = pl.program_id(1) * tm + lax.broadcasted_iota(jnp.int32, (tm, LANES), 0)
    blk_hot = jnp.where(lane - HEAD_DIM == jnp.right_shift(tok, int(np.log2(SEL_BLOCK))), 1.0, 0.0)
    h = (_rms(x) * g_ref[...]) * (1.0 + mod_ref[1:2, :]) + mod_ref[0:1, :]
    hb = h.astype(BF16)

    def seg(start, width):
        return _dot(hb, w_ref[:, start:start + width])

    q_ref[0] = (seg(C_Q, D_NSA) * (HEAD_DIM ** -0.5)).astype(BF16)
    for out_ref, start in ((kc_ref, C_KC), (vc_ref, C_VC)):
        tok_ref[...] = seg(start, KV_W)
        for i in range(CMP_STRIDE):
            rows = tok_ref[pl.ds(i, tm // CMP_STRIDE, stride=CMP_STRIDE), :]
            for g in range(KV_GROUPS):
                c0 = (g * CMP_STRIDE + i) * HEAD_DIM
                out_ref[0, :, c0:c0 + HEAD_DIM] = rows[:, g * HEAD_DIM:(g + 1) * HEAD_DIM]
    lo = lane < HEAD_DIM
    swap = lambda a: pltpu.roll(a, HEAD_DIM, axis=1)
    ks, vs, kw, vw = (seg(c, KV_W) for c in (C_KS, C_VS, C_KW, C_VW))
    for g, (ks_g, kw_g, vs_g, vw_g) in enumerate(((ks, swap(kw), vs, swap(vw)), (swap(ks), kw, swap(vs), vw))):
        kk_ref[0, g] = jnp.where(lo, ks_g, kw_g).astype(BF16)
        kx_ref[0, g] = jnp.where(lo, ks_g, blk_hot).astype(BF16)
        vs_ref[0, g] = jnp.where(lo, vs_g, 1.0).astype(BF16)
        vw_ref[0, g] = jnp.where(lo, 1.0, vw_g).astype(BF16)
    gate = _sigmoid(seg(C_GATE, LANES) + gb_ref[...])
    for g in range(KV_GROUPS):
        gate_ref[0, g] = gate if g == 0 else pltpu.roll(gate, LANES - g * GATES_PER_GROUP, axis=1)
    zx_ref[0] = seg(C_ZX, D_LRU)
    zy_ref[0] = seg(C_ZY, D_LRU)


def _in_call(x, mod, norm_g, w_in_p, gate_b_p, l):
    nb, seq, _ = x.shape
    tm = ROW_TILE
    row = lambda width: pl.BlockSpec((1, tm, width), lambda b, t: (b, t, 0))
    grp = pl.BlockSpec((1, KV_GROUPS, tm, LANES), lambda b, t: (b, 0, t, 0))
    cmp_rows, cmp_width = tm // CMP_STRIDE, CMP_STRIDE * KV_W
    cmp_in = pl.BlockSpec((1, cmp_rows, cmp_width), lambda b, t: (b, t, 0))
    return pl.pallas_call(
        _in_kernel,
        grid=(nb, seq // tm),
        in_specs=[
            row(D_MODEL),
            pl.BlockSpec((None, None, 6, D_MODEL), lambda b, t: (l, b, 0, 0)),
            pl.BlockSpec((None, 1, D_MODEL), lambda b, t: (l, 0, 0)),
            pl.BlockSpec((None, D_MODEL, IN_COLS_P), lambda b, t: (l, 0, 0)),
            pl.BlockSpec((None, 1, LANES), lambda b, t: (l, 0, 0)),
        ],
        out_specs=[row(D_NSA), cmp_in, cmp_in, grp, grp, grp, grp, row(D_LRU), row(D_LRU), grp],
        out_shape=[
            jax.ShapeDtypeStruct((nb, seq, D_NSA), BF16),
            jax.ShapeDtypeStruct((nb, seq // CMP_STRIDE, cmp_width), F32),
            jax.ShapeDtypeStruct((nb, seq // CMP_STRIDE, cmp_width), F32),
            jax.ShapeDtypeStruct((nb, KV_GROUPS, seq, LANES), BF16),
            jax.ShapeDtypeStruct((nb, KV_GROUPS, seq, LANES), BF16),
            jax.ShapeDtypeStruct((nb, KV_GROUPS, seq, LANES), BF16),
            jax.ShapeDtypeStruct((nb, KV_GROUPS, seq, LANES), BF16),
            jax.ShapeDtypeStruct((nb, seq, D_LRU), F32),
            jax.ShapeDtypeStruct((nb, seq, D_LRU), F32),
            jax.ShapeDtypeStruct((nb, KV_GROUPS, seq, LANES), F32),
        ],
        scratch_shapes=[pltpu.VMEM((tm, KV_W), F32)],
        compiler_params=pltpu.CompilerParams(
            dimension_semantics=("arbitrary", "arbitrary"), vmem_limit_bytes=48 * 1024 * 1024),
        name="mixer_in",
    )(x, mod, norm_g, w_in_p, gate_b_p)


def _cmp_kernel(kc_ref, vc_ref, pos_ref, w1_ref, b1_ref, w2_ref, b2_ref, o_ref):
    def hidden(x_ref, i, g):
        width = pos_ref.shape[1]
        x = x_ref[0, :, g * width:(g + 1) * width]
        top = (x + pos_ref[2 * i:2 * i + 1, :]).astype(BF16)
        bot = (x + pos_ref[2 * i + 1:2 * i + 2, :]).astype(BF16)
        a = _dot(top, w1_ref[2 * i])
        b = _dot(bot, w1_ref[2 * i + 1])
        nrow = b.shape[0]
        b = pltpu.roll(b, nrow - 1, axis=0)
        return _gelu_tanh(a + b + b1_ref[i:i + 1, :])

    for g in range(KV_GROUPS):
        hg = jnp.concatenate([hidden(kc_ref, 0, g), hidden(vc_ref, 1, g)], axis=1).astype(BF16)
        o_ref[0, g] = (_dot(hg, w2_ref[...]) + b2_ref[...]).astype(BF16)


def _cmp_call(kc16, vc16, pos_p, w1_p, b1_p, w2_p, b2_p, l):
    nb, nrow, width = kc16.shape
    blk = pl.BlockSpec((1, nrow, width), lambda b: (b, 0, 0))
    return pl.pallas_call(
        _cmp_kernel,
        grid=(nb,),
        in_specs=[
            blk, blk,
            pl.BlockSpec((None, 4, width // KV_GROUPS), lambda b: (l, 0, 0)),
            pl.BlockSpec((None, 4, width // KV_GROUPS, CMP_HIDDEN), lambda b: (l, 0, 0, 0)),
            pl.BlockSpec((None, 2, CMP_HIDDEN), lambda b: (l, 0, 0)),
            pl.BlockSpec((None, 2 * CMP_HIDDEN, LANES), lambda b: (l, 0, 0)),
            pl.BlockSpec((None, 1, LANES), lambda b: (l, 0, 0)),
        ],
        out_specs=pl.BlockSpec((1, KV_GROUPS, nrow, LANES), lambda b: (b, 0, 0, 0)),
        out_shape=jax.ShapeDtypeStruct((nb, KV_GROUPS, nrow, LANES), BF16),
        compiler_params=pltpu.CompilerParams(
            dimension_semantics=("arbitrary",), vmem_limit_bytes=48 * 1024 * 1024),
        name="nsa_compress",
    )(kc16, vc16, pos_p, w1_p, b1_p, w2_p, b2_p)


def _attn_body(n_sel, g, j, q_ref, kx_ref, kk_ref, vs_ref, vw_ref, kvc_ref, gate_ref, ot_ref, o_ref):
    qb = QB_PER_STEP * pl.program_id(1) + j
    rows_j = slice(j * Q_BLOCK, (j + 1) * Q_BLOCK)
    nq = Q_BLOCK
    nrow = REP * nq
    q = q_ref[0, rows_j, g * REP * HEAD_DIM:(g + 1) * REP * HEAD_DIM]
    qs = jnp.concatenate([q[:, r * HEAD_DIM:(r + 1) * HEAD_DIM] for r in range(REP)], axis=0)
    zpad = jnp.zeros_like(qs)
    q_lo = jnp.concatenate([qs, zpad], axis=1)
    q_hi = jnp.concatenate([zpad, qs], axis=1)

    def tok_rows(width):
        return qb * nq + lax.broadcasted_iota(jnp.int32, (nq, width), 0)

    lane_row = lax.broadcasted_iota(jnp.int32, (nrow, LANES), 1)

    def lane_max(x):
        return functools.reduce(jnp.maximum, [x[:, i:i + LANES] for i in range(0, x.shape[1], LANES)])

    def rep_lanes(x, width):
        return jnp.concatenate([x] * (width // LANES), axis=1)

    def row_bcast(x):
        return jnp.broadcast_to(x, (x.shape[0], LANES))

    gt = gate_ref[0, g, rows_j, :]

    def gate_rows(br):
        return jnp.concatenate(
            [jnp.broadcast_to(gt[:, N_BRANCH * r + br:N_BRANCH * r + br + 1], (nq, LANES)) for r in range(REP)],
            axis=0)

    kvc = kvc_ref[0, g]
    ncb = kvc.shape[0]
    sc = _dot_nt(q_lo, kvc).reshape(REP, nq, ncb)
    cidx = lax.broadcasted_iota(jnp.int32, (nq, ncb), 1)
    mask_c = (cidx * CMP_STRIDE + (CMP_LEN - 1)) <= tok_rows(ncb)
    sc = jnp.where(mask_c[None], sc, NEG_INF)
    mc = jnp.max(sc, axis=-1, keepdims=True)
    pc = jnp.where(mask_c[None], jnp.exp(sc - mc), 0.0)
    lc = jnp.sum(pc, axis=-1, keepdims=True)
    pc = pc / jnp.where(lc > 0.0, lc, 1.0)
    o_cmp = _dot(pc.reshape(nrow, ncb).astype(BF16), kvc)

    psum = pc[0] + pc[1] + pc[2] + pc[3]
    p_hi = psum.astype(BF16)
    p_lo = (psum - p_hi.astype(F32)).astype(BF16)
    ot = ot_ref[...]
    imp = _dot_nt(ot, p_hi) + _dot_nt(ot, p_lo)
    nblk = imp.shape[0]
    jj = lax.broadcasted_iota(jnp.int32, (nblk, nq), 0)
    lane = lax.broadcasted_iota(jnp.int32, (nblk, nq), 1)
    qblk = jnp.right_shift(qb * nq + lane, int(np.log2(SEL_BLOCK)))
    valid = jj <= qblk
    forced = (jj == 0) | (jj == qblk) | (jj == qblk - 1)
    imp = jnp.where(valid, imp + jnp.where(forced, FORCE_BONUS, 0.0), NEG_INF)
    rank = jnp.zeros((nblk, nq), F32)
    for i in range(nblk):
        row = imp[i:i + 1, :]
        beats = (row > imp) | ((row == imp) & (jj > i))
        rank = rank + jnp.where(beats, 1.0, 0.0)
    sel_t = jnp.where(rank < float(SEL_TOP_K), 1.0, 0.0)
    sel_t = jnp.concatenate(
        [sel_t, jnp.ones((HEAD_DIM - nblk, nq), F32), jnp.zeros((LANES - HEAD_DIM, nq), F32)], axis=0)
    selb = ((sel_t.T[:, :HEAD_DIM] - 1.0) * -NEG_INF).astype(BF16)
    q_sel = jnp.concatenate([qs, jnp.concatenate([selb] * REP, axis=0)], axis=1)

    n_win = WINDOW // KEY_CHUNK + (0 if n_sel == 1 else 1)
    diag = qb // (KEY_CHUNK // nq)
    k0_win, ok_win = [], []
    for i in range(n_win):
        cw = diag - (n_win - 1) + i
        k0_win.append(pl.multiple_of(jnp.maximum(cw, 0) * KEY_CHUNK, KEY_CHUNK))
        kpos = cw * KEY_CHUNK + lax.broadcasted_iota(jnp.int32, (nq, KEY_CHUNK), 1)
        tq = tok_rows(KEY_CHUNK)
        ok_win.append((kpos <= tq) & (kpos > tq - WINDOW) & (kpos >= 0))
    half = nrow // 2
    acc_halves = []
    for hh in range(2):
        qh = q_hi[hh * half:(hh + 1) * half]
        s_win = []
        for k0, ok in zip(k0_win, ok_win):
            s = _dot_nt(qh, kk_ref[0, g, pl.ds(k0, KEY_CHUNK), :]).reshape(REP // 2, nq, KEY_CHUNK)
            s_win.append(jnp.where(ok[None], s, NEG_INF).reshape(half, KEY_CHUNK))
        m_w = row_bcast(jnp.max(functools.reduce(jnp.maximum, [lane_max(s) for s in s_win]), axis=1, keepdims=True))
        acc_halves.append(functools.reduce(jnp.add, [
            _dot(jnp.exp(s - rep_lanes(m_w, KEY_CHUNK)).astype(BF16), vw_ref[0, g, pl.ds(k0, KEY_CHUNK), :])
            for s, k0 in zip(s_win, k0_win)]))
    acc_w = jnp.concatenate(acc_halves, axis=0)
    o_hi = gate_rows(2) * acc_w / pltpu.roll(acc_w, HEAD_DIM, axis=1) + gate_rows(0) * o_cmp

    sel_parts = 1 if n_sel == 1 else 2
    part = nrow // sel_parts
    acc_parts = []
    for hh in range(sel_parts):
        qh = q_sel[hh * part:(hh + 1) * part]
        s_sel = []
        for c in range(n_sel):
            s = _dot_nt(qh, kx_ref[0, g, c * SEL_CHUNK:(c + 1) * SEL_CHUNK, :])
            if c == n_sel - 1:
                kpos = c * SEL_CHUNK + lax.broadcasted_iota(jnp.int32, (nq, SEL_CHUNK), 1)
                ok = kpos <= tok_rows(SEL_CHUNK)
                s = jnp.where(ok[None], s.reshape(part // nq, nq, SEL_CHUNK), NEG_INF).reshape(part, SEL_CHUNK)
            s_sel.append(s)
        m_s = row_bcast(jnp.max(functools.reduce(jnp.maximum, [lane_max(s) for s in s_sel]), axis=1, keepdims=True))
        acc_parts.append(functools.reduce(jnp.add, [
            _dot(jnp.exp(s - rep_lanes(m_s, SEL_CHUNK)).astype(BF16),
                 vs_ref[0, g, c * SEL_CHUNK:(c + 1) * SEL_CHUNK, :])
            for c, s in enumerate(s_sel)]))
    acc_s = jnp.concatenate(acc_parts, axis=0)
    o_lo = gate_rows(1) * acc_s / pltpu.roll(acc_s, HEAD_DIM, axis=1)

    t = jnp.where(lane_row < HEAD_DIM, o_lo, o_hi)
    u = t + pltpu.roll(t, HEAD_DIM, axis=1)
    lane_q = lax.broadcasted_iota(jnp.int32, (nq, LANES), 1)
    o_ref[0, rows_j, g * REP * HEAD_DIM:(g + 1) * REP * HEAD_DIM] = jnp.concatenate(
        [jnp.where(lane_q < HEAD_DIM, u[(2 * i) * nq:(2 * i + 1) * nq], u[(2 * i + 1) * nq:(2 * i + 2) * nq])
         for i in range(REP // 2)], axis=1)


def _attn_kernel(*refs):
    chunks = pl.program_id(1) // (SEL_CHUNK // (QB_PER_STEP * Q_BLOCK)) + 1

    def all_blocks(n_sel):
        for j in range(QB_PER_STEP):
            for g in range(KV_GROUPS):
                _attn_body(n_sel, g, j, *refs)

    for n_sel in range(1, refs[1].shape[2] // SEL_CHUNK + 1):
        pl.when(chunks == n_sel)(functools.partial(all_blocks, n_sel))


def _attn_call(q, kx, kk, vs, vw, kvc, gate, overlap_t):
    nb, seq, _ = q.shape
    nq = seq // Q_BLOCK
    ncb = kvc.shape[2]
    per_b = lambda rows: pl.BlockSpec((1, KV_GROUPS, rows, LANES), lambda b, i: (b, 0, 0, 0))
    return pl.pallas_call(
        _attn_kernel,
        grid=(nb, nq // QB_PER_STEP),
        in_specs=[
            pl.BlockSpec((1, QB_PER_STEP * Q_BLOCK, D_NSA), lambda b, i: (b, i, 0)),
            per_b(seq), per_b(seq), per_b(seq), per_b(seq), per_b(ncb),
            pl.BlockSpec((1, KV_GROUPS, QB_PER_STEP * Q_BLOCK, LANES), lambda b, i: (b, 0, i, 0)),
            pl.BlockSpec(overlap_t.shape, lambda b, i: (0, 0)),
        ],
        out_specs=pl.BlockSpec((1, QB_PER_STEP * Q_BLOCK, D_NSA), lambda b, i: (b, i, 0)),
        out_shape=jax.ShapeDtypeStruct((nb, seq, D_NSA), F32),
        compiler_params=pltpu.CompilerParams(
            dimension_semantics=("arbitrary", "arbitrary"),
            vmem_limit_bytes=48 * 1024 * 1024),
        name="nsa_attention",
    )(q, kx, kk, vs, vw, kvc, gate, overlap_t)


def _lru_kernel(zx_ref, zy_ref, cw_ref, cb_ref, w_ref, bias_ref, lam_ref, o_ref, hist_ref, h_ref):
    t = pl.program_id(2)
    tl = zx_ref.shape[1]
    ng = tl // HIST
    taps = LRU_CONV_W - 1

    @pl.when(t == 0)
    def _():
        hist_ref[...] = jnp.zeros(hist_ref.shape, F32)
        h_ref[...] = jnp.zeros(h_ref.shape, F32)

    xs = [zx_ref[0, pl.ds(j, HIST, stride=ng), :] for j in range(ng)]
    sub = lax.broadcasted_iota(jnp.int32, (HIST, LANES), 0)
    xe = [jnp.where(sub == 0, hist_ref[k:k + 1, :], pltpu.roll(xs[ng - taps + k], 1, axis=0))
          for k in range(taps)] + xs
    for k in range(taps):
        hist_ref[k:k + 1, :] = xs[ng - taps + k][HIST - 1:HIST, :]
    cw = [jnp.broadcast_to(cw_ref[k:k + 1, :], (HIST, LANES)) for k in range(LRU_CONV_W)]
    cb = jnp.broadcast_to(cb_ref[...], (HIST, LANES))
    xc = jnp.concatenate(
        [cb + functools.reduce(jnp.add, [cw[k] * xe[j + k] for k in range(LRU_CONV_W)]) for j in range(ng)], axis=0)

    y = _dot(xc.astype(BF16), w_ref[...])
    r = _sigmoid(y[:, :LANES] + bias_ref[0:1, :])
    i = _sigmoid(y[:, LANES:] + bias_ref[1:2, :])
    lam = lam_ref[...]
    log_sig = -(jnp.maximum(-lam, 0.0) + jnp.log1p(jnp.exp(-jnp.abs(lam))))
    log_a = r * (LRU_C * log_sig)
    a = jnp.exp(log_a)
    u = jnp.sqrt(-jnp.tanh(log_a) * (a * a + 1.0)) * (i * xc)

    hs, ps = [u[:HIST]], [a[:HIST]]
    for j in range(1, ng):
        aj = a[j * HIST:(j + 1) * HIST]
        hs.append(aj * hs[-1] + u[j * HIST:(j + 1) * HIST])
        ps.append(aj * ps[-1])
    c = h_ref[0:1, :]
    carry = [c]
    for rr in range(HIST - 1):
        c = ps[-1][rr:rr + 1, :] * c + hs[-1][rr:rr + 1, :]
        carry.append(c)
    h_ref[0:1, :] = ps[-1][HIST - 1:HIST, :] * c + hs[-1][HIST - 1:HIST, :]
    carry = jnp.concatenate(carry, axis=0)
    for j in range(ng):
        h = hs[j] + ps[j] * carry
        o_ref[0, pl.ds(j, HIST, stride=ng), :] = h * _gelu_tanh(zy_ref[0, pl.ds(j, HIST, stride=ng), :])


def _lru_call(zx, zy, conv_w, conv_b, w_ax, bias_ax, lam, l):
    nb, seq, _ = zx.shape
    tl = LRU_TILE
    row = pl.BlockSpec((1, tl, LANES), lambda b, c, t: (b, t, c))
    par = lambda rows: pl.BlockSpec((None, rows, LANES), lambda b, c, t: (l, 0, c))
    return pl.pallas_call(
        _lru_kernel,
        grid=(nb, D_LRU // LANES, seq // tl),
        in_specs=[
            row, row, par(LRU_CONV_W), par(1),
            pl.BlockSpec((None, None, LANES, 2 * LANES), lambda b, c, t: (l, c, 0, 0)),
            par(2), par(1),
        ],
        out_specs=row,
        out_shape=jax.ShapeDtypeStruct((nb, seq, D_LRU), F32),
        scratch_shapes=[pltpu.VMEM((HIST, LANES), F32), pltpu.VMEM((HIST, LANES), F32)],
        compiler_params=pltpu.CompilerParams(dimension_semantics=("arbitrary", "arbitrary", "arbitrary")),
        name="rglru",
    )(zx, zy, conv_w, conv_b, w_ax, bias_ax, lam)


def _ffn_kernel(x_ref, on_ref, ol_ref, mod_ref, gn_ref, gl_ref, wo_ref, g_ref, wg_ref, wu_ref, cw_ref, cb_ref,
                wd_ref, fg_ref, o_ref, hist_ref, act_ref, *, final_norm):
    t = pl.program_id(1)
    tm = x_ref.shape[1]

    @pl.when(t == 0)
    def _():
        hist_ref[...] = jnp.zeros(hist_ref.shape, F32)

    a = (_rms(on_ref[0]) * gn_ref[...]).astype(BF16)
    b = (_rms(ol_ref[0]) * gl_ref[...]).astype(BF16)
    x = x_ref[0] + mod_ref[2:3, :] * (_dot(a, wo_ref[:D_NSA, :]) + _dot(b, wo_ref[D_NSA:, :]))
    h = (_rms(x) * g_ref[...]) * (1.0 + mod_ref[4:5, :]) + mod_ref[3:4, :]
    hb = h.astype(BF16)
    for j in range(D_FF // FF_CHUNK):
        cs = slice(j * FF_CHUNK, (j + 1) * FF_CHUNK)
        gp = _dot(hb, wg_ref[:, cs])
        ge = jnp.concatenate([hist_ref[:, cs], gp], axis=0)
        hist_ref[:, cs] = gp[tm - HIST:, :]
        gate = cb_ref[:, cs] + gp * cw_ref[FFN_CONV_W - 1:FFN_CONV_W, cs]
        for k in range(FFN_CONV_W - 1):
            off = HIST - (FFN_CONV_W - 1) + k
            gate = gate + ge[off:off + tm, :] * cw_ref[k:k + 1, cs]
        act = (gate * _sigmoid(gate)) * _dot(hb, wu_ref[:, cs])
        act_ref[:, cs] = act.astype(BF16)
    y = x + mod_ref[5:6, :] * _dot(act_ref[...], wd_ref[...])
    if final_norm:
        y = _rms(y) * fg_ref[...]
    o_ref[0] = y


def _ffn_call(x, o_nsa, o_lru, mod, g_nsa, g_lru, w_out, norm_g, w_gate, w_up, conv_w, conv_b, w_down, final_g,
              l, final_norm):
    nb, seq, _ = x.shape
    tm = ROW_TILE
    row = pl.BlockSpec((1, tm, D_MODEL), lambda b, t: (b, t, 0))
    half_row = pl.BlockSpec((1, tm, D_NSA), lambda b, t: (b, t, 0))
    once = dict(pipeline_mode=pl.Buffered(1))
    return pl.pallas_call(
        functools.partial(_ffn_kernel, final_norm=final_norm),
        grid=(nb, seq // tm),
        in_specs=[
            row, half_row, half_row,
            pl.BlockSpec((None, None, 6, D_MODEL), lambda b, t: (l, b, 0, 0)),
            pl.BlockSpec((None, 1, D_NSA), lambda b, t: (l, 0, 0)),
            pl.BlockSpec((None, 1, D_LRU), lambda b, t: (l, 0, 0)),
            pl.BlockSpec((None, D_MODEL, D_MODEL), lambda b, t: (l, 0, 0), **once),
            pl.BlockSpec((None, 1, D_MODEL), lambda b, t: (l, 0, 0)),
            pl.BlockSpec((None, D_MODEL, D_FF), lambda b, t: (l, 0, 0), **once),
            pl.BlockSpec((None, D_MODEL, D_FF), lambda b, t: (l, 0, 0), **once),
            pl.BlockSpec((None, FFN_CONV_W, D_FF), lambda b, t: (l, 0, 0)),
            pl.BlockSpec((None, 1, D_FF), lambda b, t: (l, 0, 0)),
            pl.BlockSpec((None, D_FF, D_MODEL), lambda b, t: (l, 0, 0), **once),
            pl.BlockSpec((1, D_MODEL), lambda b, t: (0, 0)),
        ],
        out_specs=row,
        out_shape=jax.ShapeDtypeStruct(x.shape, F32),
        scratch_shapes=[pltpu.VMEM((HIST, D_FF), F32), pltpu.VMEM((tm, D_FF), BF16)],
        compiler_params=pltpu.CompilerParams(
            dimension_semantics=("arbitrary", "arbitrary"), vmem_limit_bytes=56 * 1024 * 1024),
        name="conv_ffn",
    )(x, o_nsa, o_lru, mod, g_nsa, g_lru, w_out, norm_g, w_gate, w_up, conv_w, conv_b, w_down, final_g)


def _prep_in(w_in, gate_b):
    n_gate = N_BRANCH * HEADS
    zg0 = C_ZX
    w_p = jnp.concatenate([
        w_in[..., :zg0], w_in[..., zg0 + n_gate:], w_in[..., zg0:zg0 + n_gate],
        jnp.zeros(w_in.shape[:-1] + (LANES - n_gate,), w_in.dtype)], axis=-1).astype(BF16)
    assert w_p.shape[-1] == IN_COLS_P
    gb = jnp.pad(gate_b, ((0, 0), (0, LANES - n_gate))).reshape(DEPTH, 1, LANES)
    return w_p, gb


def _prep_cmp(cmp_pos, cmp_w1, cmp_b1, cmp_w2, cmp_b2):
    half_w = (CMP_LEN // 2) * HEAD_DIM
    pos_p = cmp_pos.reshape(DEPTH, 4, half_w)
    w1_p = cmp_w1.reshape(DEPTH, 4, half_w, CMP_HIDDEN).astype(BF16)
    b1_p = cmp_b1
    zeros = jnp.zeros((DEPTH, CMP_HIDDEN, HEAD_DIM), cmp_w2.dtype)
    w2_p = jnp.concatenate([
        jnp.concatenate([cmp_w2[:, 0], zeros], axis=-1),
        jnp.concatenate([zeros, cmp_w2[:, 1]], axis=-1)], axis=1).astype(BF16)
    b2_p = cmp_b2.reshape(DEPTH, 1, 2 * HEAD_DIM)
    return pos_p, w1_p, b1_p, w2_p, b2_p


def _prep_lru(lru_wa, lru_wx, lru_ba, lru_bx):
    per_tile = LANES // LRU_BLOCK_W
    assert per_tile == 2

    def tiles(w):
        w = w.reshape(DEPTH, LRU_BLOCKS // per_tile, per_tile, LRU_BLOCK_W, LRU_BLOCK_W)
        zero = jnp.zeros_like(w[:, :, 0])
        return jnp.concatenate([jnp.concatenate([w[:, :, 0], zero], axis=-1),
                                jnp.concatenate([zero, w[:, :, 1]], axis=-1)], axis=-2)

    w_ax = jnp.concatenate([tiles(lru_wa), tiles(lru_wx)], axis=-1).astype(BF16)
    bias_ax = jnp.stack([lru_ba, lru_bx], axis=1)
    return w_ax, bias_ax


def _attn_consts(seq):
    nblk = seq // SEL_BLOCK
    ncb_p = seq // CMP_STRIDE
    cstart = np.arange(ncb_p) * CMP_STRIDE
    js = np.arange(nblk) * SEL_BLOCK
    overlap_t = ((cstart[None, :] < js[:, None] + SEL_BLOCK) & (cstart[None, :] + CMP_LEN > js[:, None]))
    return jnp.asarray(overlap_t.astype(np.float32), BF16)


def kernel(x, c, ada_w, ada_b, mix_norm_g, ffn_norm_g, w_in, nsa_gate_b, cmp_pos, cmp_w1, cmp_b1, cmp_w2, cmp_b2, lru_conv_w, lru_conv_b, lru_wa, lru_ba, lru_wx, lru_bx, lru_lambda, nsa_out_norm_g, lru_out_norm_g, w_out, ffn_w_gate, ffn_w_up, ffn_conv_w, ffn_conv_b, ffn_w_down, final_norm_g):
    nb, seq, _ = x.shape
    mod = _mod_call(c, ada_w, ada_b)
    w_in_p, gate_b_p = _prep_in(w_in, nsa_gate_b)
    pos_p, w1_p, b1_p, w2_p, b2_p = _prep_cmp(cmp_pos, cmp_w1, cmp_b1, cmp_w2, cmp_b2)
    w_ax, bias_ax = _prep_lru(lru_wa, lru_wx, lru_ba, lru_bx)
    overlap_t = _attn_consts(seq)
    w_out_b = w_out.astype(BF16)
    wg_b, wu_b, wd_b = ffn_w_gate.astype(BF16), ffn_w_up.astype(BF16), ffn_w_down.astype(BF16)
    row3 = lambda a: a.reshape(DEPTH, 1, a.shape[-1])
    mix_g, ffn_g = row3(mix_norm_g), row3(ffn_norm_g)
    nsa_g, lru_g = row3(nsa_out_norm_g), row3(lru_out_norm_g)
    lru_cb, lam, ffn_cb = row3(lru_conv_b), row3(lru_lambda), row3(ffn_conv_b)
    final_g = final_norm_g.reshape(1, D_MODEL)

    for l in range(DEPTH):
        q, kc, vc, kx, kk, vs, vw, zx, zy, gate = _in_call(x, mod, mix_g, w_in_p, gate_b_p, l)
        kvc = _cmp_call(kc, vc, pos_p, w1_p, b1_p, w2_p, b2_p, l)
        o_nsa = _attn_call(q, kx, kk, vs, vw, kvc, gate, overlap_t)
        o_lru = _lru_call(zx, zy, lru_conv_w, lru_cb, w_ax, bias_ax, lam, l)
        x = _ffn_call(x, o_nsa, o_lru, mod, nsa_g, lru_g, w_out_b, ffn_g, wg_b, wu_b, ffn_conv_w, ffn_cb, wd_b,
                      final_g, l, l == DEPTH - 1)
    return x
```

```python
import functools

import numpy as np
import jax
import jax.numpy as jnp
from jax import lax
from jax.experimental import pallas as pl
from jax.experimental.pallas import tpu as pltpu

F32 = jnp.float32
BF16 = jnp.bfloat16

D_MODEL = 1024
DEPTH = 4
D_NSA = 512
D_LRU = 512
HEADS = 8
HEAD_DIM = 64
KV_GROUPS = 2
REP = HEADS // KV_GROUPS
N_BRANCH = 3
CMP_LEN = 32
CMP_STRIDE = 16
CMP_HIDDEN = 4 * HEAD_DIM
SEL_BLOCK = 64
SEL_TOP_K = 8
WINDOW = 512
Q_BLOCK = 128
LRU_BLOCKS = 8
LRU_BLOCK_W = D_LRU // LRU_BLOCKS
LRU_CONV_W = 4
LRU_C = 8.0
D_FF = 2816
FFN_CONV_W = 3
NORM_EPS = 1e-6
NEG_INF = -1e30
FORCE_BONUS = 1e4

KV_W = KV_GROUPS * HEAD_DIM
IN_COLS = D_NSA + 6 * KV_W + N_BRANCH * HEADS + 2 * D_LRU
LANES = 128
KEY_CHUNK = 256
SEL_CHUNK = 512
QB_PER_STEP = 2
assert SEL_CHUNK % (QB_PER_STEP * Q_BLOCK) == 0
ROW_TILE = 512
LRU_TILE = 1024
FF_CHUNK = 256
HIST = 8

C_Q = 0
C_KC = C_Q + D_NSA
C_VC = C_KC + KV_W
C_KS, C_VS, C_KW, C_VW = (C_VC + (i + 1) * KV_W for i in range(4))
C_ZX = C_VW + KV_W
C_ZY = C_ZX + D_LRU
C_GATE = C_ZY + D_LRU
IN_COLS_P = C_GATE + LANES
GATES_PER_GROUP = N_BRANCH * REP


def _dot(a, b):
    return jnp.dot(a, b, preferred_element_type=F32)


def _dot_nt(a, b):
    return lax.dot_general(a, b, (((1,), (1,)), ((), ())), preferred_element_type=F32)


def _sigmoid(x):
    return 0.5 * jnp.tanh(0.5 * x) + 0.5


def _gelu_tanh(x):
    c = np.float32(np.sqrt(2.0 / np.pi))
    return x * (0.5 * (1.0 + jnp.tanh(c * (x + 0.044715 * (x * x * x)))))


def _rms(x):
    return x * lax.rsqrt(jnp.mean(x * x, axis=-1, keepdims=True) + NORM_EPS)


def _mod_kernel(c_ref, w_ref, b_ref, o_ref):
    c = c_ref[...]
    ca = (c * _sigmoid(c)).astype(BF16)
    o_ref[...] = _dot(ca, w_ref[...].astype(BF16)) + b_ref[...]


def _mod_call(c, ada_w, ada_b):
    nb = c.shape[0]
    out = pl.pallas_call(
        _mod_kernel,
        grid=(DEPTH, 6),
        in_specs=[
            pl.BlockSpec((nb, D_MODEL), lambda l, j: (0, 0)),
            pl.BlockSpec((None, D_MODEL, D_MODEL), lambda l, j: (l, 0, j)),
            pl.BlockSpec((None, 1, D_MODEL), lambda l, j: (l, 0, j)),
        ],
        out_specs=pl.BlockSpec((None, nb, D_MODEL), lambda l, j: (l, 0, j)),
        out_shape=jax.ShapeDtypeStruct((DEPTH, nb, 6 * D_MODEL), F32),
        name="adaln_mod",
    )(c, ada_w, ada_b.reshape(DEPTH, 1, 6 * D_MODEL))
    return out.reshape(DEPTH, nb, 6, D_MODEL)


def _in_kernel(x_ref, mod_ref, g_ref, w_ref, gb_ref,
               q_ref, kc_ref, vc_ref, kx_ref, kk_ref, vs_ref, vw_ref, zx_ref, zy_ref, gate_ref, tok_ref):
    x = x_ref[0]
    tm = x.shape[0]
    lane = lax.broadcasted_iota(jnp.int32, (tm, LANES), 1)
    tok = pl.program_id(1) * tm + lax.broadcasted_iota(jnp.int32, (tm, LANES), 0)
    blk_hot = jnp.where(lane - HEAD_DIM == jnp.right_shift(tok, int(np.log2(SEL_BLOCK))), 1.0, 0.0)
    h = (_rms(x) * g_ref[...]) * (1.0 + mod_ref[1:2, :]) + mod_ref[0:1, :]
    hb = h.astype(BF16)

    def seg(start, width):
        return _dot(hb, w_ref[:, start:start + width])

    q_ref[0] = (seg(C_Q, D_NSA) * (HEAD_DIM ** -0.5)).astype(BF16)
    for out_ref, start in ((kc_ref, C_KC), (vc_ref, C_VC)):
        tok_ref[...] = seg(start, KV_W)
        for i in range(CMP_STRIDE):
            rows = tok_ref[pl.ds(i, tm // CMP_STRIDE, stride=CMP_STRIDE), :]
            for g in range(KV_GROUPS):
                c0 = (g * CMP_STRIDE + i) * HEAD_DIM
                out_ref[0, :, c0:c0 + HEAD_DIM] = rows[:, g * HEAD_DIM:(g + 1) * HEAD_DIM]
    lo = lane < HEAD_DIM
    swap = lambda a: pltpu.roll(a, HEAD_DIM, axis=1)
    ks, vs, kw, vw = (seg(c, KV_W) for c in (C_KS, C_VS, C_KW, C_VW))
    for g, (ks_g, kw_g, vs_g, vw_g) in enumerate(((ks, swap(kw), vs, swap(vw)), (swap(ks), kw, swap(vs), vw))):
        kk_ref[0, g] = jnp.where(lo, ks_g, kw_g).astype(BF16)
        kx_ref[0, g] = jnp.where(lo, ks_g, blk_hot).astype(BF16)
        vs_ref[0, g] = jnp.where(lo, vs_g, 1.0).astype(BF16)
        vw_ref[0, g] = jnp.where(lo, 1.0, vw_g).astype(BF16)
    gate = _sigmoid(seg(C_GATE, LANES) + gb_ref[...])
    for g in range(KV_GROUPS):
        gate_ref[0, g] = gate if g == 0 else pltpu.roll(gate, LANES - g * GATES_PER_GROUP, axis=1)
    zx_ref[0] = seg(C_ZX, D_LRU)
    zy_ref[0] = seg(C_ZY, D_LRU)


def _in_call(x, mod, norm_g, w_in_p, gate_b_p, l):
    nb, seq, _ = x.shape
    tm = ROW_TILE
    row = lambda width: pl.BlockSpec((1, tm, width), lambda b, t: (b, t, 0))
    grp = pl.BlockSpec((1, KV_GROUPS, tm, LANES), lambda b, t: (b, 0, t, 0))
    cmp_rows, cmp_width = tm // CMP_STRIDE, CMP_STRIDE * KV_W
    cmp_in = pl.BlockSpec((1, cmp_rows, cmp_width), lambda b, t: (b, t, 0))
    return pl.pallas_call(
        _in_kernel,
        grid=(nb, seq // tm),
        in_specs=[
            row(D_MODEL),
            pl.BlockSpec((None, None, 6, D_MODEL), lambda b, t: (l, b, 0, 0)),
            pl.BlockSpec((None, 1, D_MODEL), lambda b, t: (l, 0, 0)),
            pl.BlockSpec((None, D_MODEL, IN_COLS_P), lambda b, t: (l, 0, 0)),
            pl.BlockSpec((None, 1, LANES), lambda b, t: (l, 0, 0)),
        ],
        out_specs=[row(D_NSA), cmp_in, cmp_in, grp, grp, grp, grp, row(D_LRU), row(D_LRU), grp],
        out_shape=[
            jax.ShapeDtypeStruct((nb, seq, D_NSA), BF16),
            jax.ShapeDtypeStruct((nb, seq // CMP_STRIDE, cmp_width), F32),
            jax.ShapeDtypeStruct((nb, seq // CMP_STRIDE, cmp_width), F32),
            jax.ShapeDtypeStruct((nb, KV_GROUPS, seq, LANES), BF16),
            jax.ShapeDtypeStruct((nb, KV_GROUPS, seq, LANES), BF16),
            jax.ShapeDtypeStruct((nb, KV_GROUPS, seq, LANES), BF16),
            jax.ShapeDtypeStruct((nb, KV_GROUPS, seq, LANES), BF16),
            jax.ShapeDtypeStruct((nb, seq, D_LRU), F32),
            jax.ShapeDtypeStruct((nb, seq, D_LRU), F32),
            jax.ShapeDtypeStruct((nb, KV_GROUPS, seq, LANES), F32),
        ],
        scratch_shapes=[pltpu.VMEM((tm, KV_W), F32)],
        compiler_params=pltpu.CompilerParams(
            dimension_semantics=("arbitrary", "arbitrary"), vmem_limit_bytes=48 * 1024 * 1024),
        name="mixer_in",
    )(x, mod, norm_g, w_in_p, gate_b_p)


def _cmp_kernel(kc_ref, vc_ref, pos_ref, w1_ref, b1_ref, w2_ref, b2_ref, o_ref):
    def hidden(x_ref, i, g):
        width = pos_ref.shape[1]
        x = x_ref[0, :, g * width:(g + 1) * width]
        top = (x + pos_ref[2 * i:2 * i + 1, :]).astype(BF16)
        bot = (x + pos_ref[2 * i + 1:2 * i + 2, :]).astype(BF16)
        a = _dot(top, w1_ref[2 * i])
        b = _dot(bot, w1_ref[2 * i + 1])
        nrow = b.shape[0]
        b = pltpu.roll(b, nrow - 1, axis=0)
        return _gelu_tanh(a + b + b1_ref[i:i + 1, :])

    for g in range(KV_GROUPS):
        hg = jnp.concatenate([hidden(kc_ref, 0, g), hidden(vc_ref, 1, g)], axis=1).astype(BF16)
        o_ref[0, g] = (_dot(hg, w2_ref[...]) + b2_ref[...]).astype(BF16)


def _cmp_call(kc16, vc16, pos_p, w1_p, b1_p, w2_p, b2_p, l):
    nb, nrow, width = kc16.shape
    blk = pl.BlockSpec((1, nrow, width), lambda b: (b, 0, 0))
    return pl.pallas_call(
        _cmp_kernel,
        grid=(nb,),
        in_specs=[
            blk, blk,
            pl.BlockSpec((None, 4, width // KV_GROUPS), lambda b: (l, 0, 0)),
            pl.BlockSpec((None, 4, width // KV_GROUPS, CMP_HIDDEN), lambda b: (l, 0, 0, 0)),
            pl.BlockSpec((None, 2, CMP_HIDDEN), lambda b: (l, 0, 0)),
            pl.BlockSpec((None, 2 * CMP_HIDDEN, LANES), lambda b: (l, 0, 0)),
            pl.BlockSpec((None, 1, LANES), lambda b: (l, 0, 0)),
        ],
        out_specs=pl.BlockSpec((1, KV_GROUPS, nrow, LANES), lambda b: (b, 0, 0, 0)),
        out_shape=jax.ShapeDtypeStruct((nb, KV_GROUPS, nrow, LANES), BF16),
        compiler_params=pltpu.CompilerParams(
            dimension_semantics=("arbitrary",), vmem_limit_bytes=48 * 1024 * 1024),
        name="nsa_compress",
    )(kc16, vc16, pos_p, w1_p, b1_p, w2_p, b2_p)


def _attn_body(n_sel, g, j, q_ref, kx_ref, kk_ref, vs_ref, vw_ref, kvc_ref, gate_ref, ot_ref, o_ref):
    qb = QB_PER_STEP * pl.program_id(1) + j
    rows_j = slice(j * Q_BLOCK, (j + 1) * Q_BLOCK)
    nq = Q_BLOCK
    nrow = REP * nq
    q = q_ref[0, rows_j, g * REP * HEAD_DIM:(g + 1) * REP * HEAD_DIM]
    qs = jnp.concatenate([q[:, r * HEAD_DIM:(r + 1) * HEAD_DIM] for r in range(REP)], axis=0)
    zpad = jnp.zeros_like(qs)
    q_lo = jnp.concatenate([qs, zpad], axis=1)
    q_hi = jnp.concatenate([zpad, qs], axis=1)

    def tok_rows(width):
        return qb * nq + lax.broadcasted_iota(jnp.int32, (nq, width), 0)

    lane_row = lax.broadcasted_iota(jnp.int32, (nrow, LANES), 1)

    def lane_max(x):
        return functools.reduce(jnp.maximum, [x[:, i:i + LANES] for i in range(0, x.shape[1], LANES)])

    def rep_lanes(x, width):
        return jnp.concatenate([x] * (width // LANES), axis=1)

    def row_bcast(x):
        return jnp.broadcast_to(x, (x.shape[0], LANES))

    gt = gate_ref[0, g, rows_j, :]

    def gate_rows(br):
        return jnp.concatenate(
            [jnp.broadcast_to(gt[:, N_BRANCH * r + br:N_BRANCH * r + br + 1], (nq, LANES)) for r in range(REP)],
            axis=0)

    kvc = kvc_ref[0, g]
    ncb = kvc.shape[0]
    sc = _dot_nt(q_lo, kvc).reshape(REP, nq, ncb)
    cidx = lax.broadcasted_iota(jnp.int32, (nq, ncb), 1)
    mask_c = (cidx * CMP_STRIDE + (CMP_LEN - 1)) <= tok_rows(ncb)
    sc = jnp.where(mask_c[None], sc, NEG_INF)
    mc = jnp.max(sc, axis=-1, keepdims=True)
    pc = jnp.where(mask_c[None], jnp.exp(sc - mc), 0.0)
    lc = jnp.sum(pc, axis=-1, keepdims=True)
    pc = pc / jnp.where(lc > 0.0, lc, 1.0)
    o_cmp = _dot(pc.reshape(nrow, ncb).astype(BF16), kvc)

    psum = pc[0] + pc[1] + pc[2] + pc[3]
    p_hi = psum.astype(BF16)
    p_lo = (psum - p_hi.astype(F32)).astype(BF16)
    ot = ot_ref[...]
    imp = _dot_nt(ot, p_hi) + _dot_nt(ot, p_lo)
    nblk = imp.shape[0]
    jj = lax.broadcasted_iota(jnp.int32, (nblk, nq), 0)
    lane = lax.broadcasted_iota(jnp.int32, (nblk, nq), 1)
    qblk = jnp.right_shift(qb * nq + lane, int(np.log2(SEL_BLOCK)))
    valid = jj <= qblk
    forced = (jj == 0) | (jj == qblk) | (jj == qblk - 1)
    imp = jnp.where(valid, imp + jnp.where(forced, FORCE_BONUS, 0.0), NEG_INF)
    rank = jnp.zeros((nblk, nq), F32)
    for i in range(nblk):
        row = imp[i:i + 1, :]
        beats = (row > imp) | ((row == imp) & (jj > i))
        rank = rank + jnp.where(beats, 1.0, 0.0)
    sel_t = jnp.where(rank < float(SEL_TOP_K), 1.0, 0.0)
    sel_t = jnp.concatenate(
        [sel_t, jnp.ones((HEAD_DIM - nblk, nq), F32), jnp.zeros((LANES - HEAD_DIM, nq), F32)], axis=0)
    selb = ((sel_t.T[:, :HEAD_DIM] - 1.0) * -NEG_INF).astype(BF16)
    q_sel = jnp.concatenate([qs, jnp.concatenate([selb] * REP, axis=0)], axis=1)

    n_win = WINDOW // KEY_CHUNK + (0 if n_sel == 1 else 1)
    diag = qb // (KEY_CHUNK // nq)
    k0_win, ok_win = [], []
    for i in range(n_win):
        cw = diag - (n_win - 1) + i
        k0_win.append(pl.multiple_of(jnp.maximum(cw, 0) * KEY_CHUNK, KEY_CHUNK))
        kpos = cw * KEY_CHUNK + lax.broadcasted_iota(jnp.int32, (nq, KEY_CHUNK), 1)
        tq = tok_rows(KEY_CHUNK)
        ok_win.append((kpos <= tq) & (kpos > tq - WINDOW) & (kpos >= 0))
    win_parts = 2 if n_sel == 1 else 1
    wrows = nrow // win_parts
    acc_parts_w = []
    for hh in range(win_parts):
        qh = q_hi[hh * wrows:(hh + 1) * wrows]
        s_win = []
        for k0, ok in zip(k0_win, ok_win):
            s = _dot_nt(qh, kk_ref[0, g, pl.ds(k0, KEY_CHUNK), :]).reshape(wrows // nq, nq, KEY_CHUNK)
            s_win.append(jnp.where(ok[None], s, NEG_INF).reshape(wrows, KEY_CHUNK))
        m_w = row_bcast(jnp.max(functools.reduce(jnp.maximum, [lane_max(s) for s in s_win]), axis=1, keepdims=True))
        acc_parts_w.append(functools.reduce(jnp.add, [
            _dot(jnp.exp(s - rep_lanes(m_w, KEY_CHUNK)).astype(BF16), vw_ref[0, g, pl.ds(k0, KEY_CHUNK), :])
            for s, k0 in zip(s_win, k0_win)]))
    acc_w = jnp.concatenate(acc_parts_w, axis=0)
    o_hi = gate_rows(2) * acc_w / pltpu.roll(acc_w, HEAD_DIM, axis=1) + gate_rows(0) * o_cmp

    sel_parts = 1
    part = nrow // sel_parts
    acc_parts = []
    for hh in range(sel_parts):
        qh = q_sel[hh * part:(hh + 1) * part]
        s_sel = []
        for c in range(n_sel):
            s = _dot_nt(qh, kx_ref[0, g, c * SEL_CHUNK:(c + 1) * SEL_CHUNK, :])
            if c == n_sel - 1:
                kpos = c * SEL_CHUNK + lax.broadcasted_iota(jnp.int32, (nq, SEL_CHUNK), 1)
                ok = kpos <= tok_rows(SEL_CHUNK)
                s = jnp.where(ok[None], s.reshape(part // nq, nq, SEL_CHUNK), NEG_INF).reshape(part, SEL_CHUNK)
            s_sel.append(s)
        m_s = row_bcast(jnp.max(functools.reduce(jnp.maximum, [lane_max(s) for s in s_sel]), axis=1, keepdims=True))
        acc_parts.append(functools.reduce(jnp.add, [
            _dot(jnp.exp(s - rep_lanes(m_s, SEL_CHUNK)).astype(BF16),
                 vs_ref[0, g, c * SEL_CHUNK:(c + 1) * SEL_CHUNK, :])
            for c, s in enumerate(s_sel)]))
    acc_s = jnp.concatenate(acc_parts, axis=0)
    o_lo = gate_rows(1) * acc_s / pltpu.roll(acc_s, HEAD_DIM, axis=1)

    t = jnp.where(lane_row < HEAD_DIM, o_lo, o_hi)
    u = t + pltpu.roll(t, HEAD_DIM, axis=1)
    lane_q = lax.broadcasted_iota(jnp.int32, (nq, LANES), 1)
    o_ref[0, rows_j, g * REP * HEAD_DIM:(g + 1) * REP * HEAD_DIM] = jnp.concatenate(
        [jnp.where(lane_q < HEAD_DIM, u[(2 * i) * nq:(2 * i + 1) * nq], u[(2 * i + 1) * nq:(2 * i + 2) * nq])
         for i in range(REP // 2)], axis=1)


def _attn_kernel(*refs):
    chunks = pl.program_id(1) // (SEL_CHUNK // (QB_PER_STEP * Q_BLOCK)) + 1

    def all_blocks(n_sel):
        for j in range(QB_PER_STEP):
            for g in range(KV_GROUPS):
                _attn_body(n_sel, g, j, *refs)

    for n_sel in range(1, refs[1].shape[2] // SEL_CHUNK + 1):
        pl.when(chunks == n_sel)(functools.partial(all_blocks, n_sel))


def _attn_call(q, kx, kk, vs, vw, kvc, gate, overlap_t):
    nb, seq, _ = q.shape
    nq = seq // Q_BLOCK
    ncb = kvc.shape[2]
    per_b = lambda rows: pl.BlockSpec((1, KV_GROUPS, rows, LANES), lambda b, i: (b, 0, 0, 0))
    return pl.pallas_call(
        _attn_kernel,
        grid=(nb, nq // QB_PER_STEP),
        in_specs=[
            pl.BlockSpec((1, QB_PER_STEP * Q_BLOCK, D_NSA), lambda b, i: (b, i, 0)),
            per_b(seq), per_b(seq), per_b(seq), per_b(seq), per_b(ncb),
            pl.BlockSpec((1, KV_GROUPS, QB_PER_STEP * Q_BLOCK, LANES), lambda b, i: (b, 0, i, 0)),
            pl.BlockSpec(overlap_t.shape, lambda b, i: (0, 0)),
        ],
        out_specs=pl.BlockSpec((1, QB_PER_STEP * Q_BLOCK, D_NSA), lambda b, i: (b, i, 0)),
        out_shape=jax.ShapeDtypeStruct((nb, seq, D_NSA), F32),
        compiler_params=pltpu.CompilerParams(
            dimension_semantics=("arbitrary", "arbitrary"),
            vmem_limit_bytes=48 * 1024 * 1024),
        name="nsa_attention",
    )(q, kx, kk, vs, vw, kvc, gate, overlap_t)


def _lru_kernel(zx_ref, zy_ref, cw_ref, cb_ref, w_ref, bias_ref, lam_ref, o_ref, hist_ref, h_ref):
    t = pl.program_id(2)
    tl = zx_ref.shape[1]
    ng = tl // HIST
    taps = LRU_CONV_W - 1

    @pl.when(t == 0)
    def _():
        hist_ref[...] = jnp.zeros(hist_ref.shape, F32)
        h_ref[...] = jnp.zeros(h_ref.shape, F32)

    xs = [zx_ref[0, pl.ds(j, HIST, stride=ng), :] for j in range(ng)]
    sub = lax.broadcasted_iota(jnp.int32, (HIST, LANES), 0)
    xe = [jnp.where(sub == 0, hist_ref[k:k + 1, :], pltpu.roll(xs[ng - taps + k], 1, axis=0))
          for k in range(taps)] + xs
    for k in range(taps):
        hist_ref[k:k + 1, :] = xs[ng - taps + k][HIST - 1:HIST, :]
    cw = [jnp.broadcast_to(cw_ref[k:k + 1, :], (HIST, LANES)) for k in range(LRU_CONV_W)]
    cb = jnp.broadcast_to(cb_ref[...], (HIST, LANES))
    xc = jnp.concatenate(
        [cb + functools.reduce(jnp.add, [cw[k] * xe[j + k] for k in range(LRU_CONV_W)]) for j in range(ng)], axis=0)

    y = _dot(xc.astype(BF16), w_ref[...])
    r = _sigmoid(y[:, :LANES] + bias_ref[0:1, :])
    i = _sigmoid(y[:, LANES:] + bias_ref[1:2, :])
    lam = lam_ref[...]
    log_sig = -(jnp.maximum(-lam, 0.0) + jnp.log1p(jnp.exp(-jnp.abs(lam))))
    log_a = r * (LRU_C * log_sig)
    a = jnp.exp(log_a)
    u = jnp.sqrt(-jnp.tanh(log_a) * (a * a + 1.0)) * (i * xc)

    hs, ps = [u[:HIST]], [a[:HIST]]
    for j in range(1, ng):
        aj = a[j * HIST:(j + 1) * HIST]
        hs.append(aj * hs[-1] + u[j * HIST:(j + 1) * HIST])
        ps.append(aj * ps[-1])
    c = h_ref[0:1, :]
    carry = [c]
    for rr in range(HIST - 1):
        c = ps[-1][rr:rr + 1, :] * c + hs[-1][rr:rr + 1, :]
        carry.append(c)
    h_ref[0:1, :] = ps[-1][HIST - 1:HIST, :] * c + hs[-1][HIST - 1:HIST, :]
    carry = jnp.concatenate(carry, axis=0)
    for j in range(ng):
        h = hs[j] + ps[j] * carry
        o_ref[0, pl.ds(j, HIST, stride=ng), :] = h * _gelu_tanh(zy_ref[0, pl.ds(j, HIST, stride=ng), :])


def _lru_call(zx, zy, conv_w, conv_b, w_ax, bias_ax, lam, l):
    nb, seq, _ = zx.shape
    tl = LRU_TILE
    row = pl.BlockSpec((1, tl, LANES), lambda b, c, t: (b, t, c))
    par = lambda rows: pl.BlockSpec((None, rows, LANES), lambda b, c, t: (l, 0, c))
    return pl.pallas_call(
        _lru_kernel,
        grid=(nb, D_LRU // LANES, seq // tl),
        in_specs=[
            row, row, par(LRU_CONV_W), par(1),
            pl.BlockSpec((None, None, LANES, 2 * LANES), lambda b, c, t: (l, c, 0, 0)),
            par(2), par(1),
        ],
        out_specs=row,
        out_shape=jax.ShapeDtypeStruct((nb, seq, D_LRU), F32),
        scratch_shapes=[pltpu.VMEM((HIST, LANES), F32), pltpu.VMEM((HIST, LANES), F32)],
        compiler_params=pltpu.CompilerParams(dimension_semantics=("arbitrary", "arbitrary", "arbitrary")),
        name="rglru",
    )(zx, zy, conv_w, conv_b, w_ax, bias_ax, lam)


def _ffn_kernel(x_ref, on_ref, ol_ref, mod_ref, gn_ref, gl_ref, wo_ref, g_ref, wg_ref, wu_ref, cw_ref, cb_ref,
                wd_ref, fg_ref, o_ref, hist_ref, act_ref, *, final_norm):
    t = pl.program_id(1)
    tm = x_ref.shape[1]

    @pl.when(t == 0)
    def _():
        hist_ref[...] = jnp.zeros(hist_ref.shape, F32)

    a = (_rms(on_ref[0]) * gn_ref[...]).astype(BF16)
    b = (_rms(ol_ref[0]) * gl_ref[...]).astype(BF16)
    x = x_ref[0] + mod_ref[2:3, :] * (_dot(a, wo_ref[:D_NSA, :]) + _dot(b, wo_ref[D_NSA:, :]))
    h = (_rms(x) * g_ref[...]) * (1.0 + mod_ref[4:5, :]) + mod_ref[3:4, :]
    hb = h.astype(BF16)
    for j in range(D_FF // FF_CHUNK):
        cs = slice(j * FF_CHUNK, (j + 1) * FF_CHUNK)
        gp = _dot(hb, wg_ref[:, cs])
        ge = jnp.concatenate([hist_ref[:, cs], gp], axis=0)
        hist_ref[:, cs] = gp[tm - HIST:, :]
        gate = cb_ref[:, cs] + gp * cw_ref[FFN_CONV_W - 1:FFN_CONV_W, cs]
        for k in range(FFN_CONV_W - 1):
            off = HIST - (FFN_CONV_W - 1) + k
            gate = gate + ge[off:off + tm, :] * cw_ref[k:k + 1, cs]
        act = (gate * _sigmoid(gate)) * _dot(hb, wu_ref[:, cs])
        act_ref[:, cs] = act.astype(BF16)
    y = x + mod_ref[5:6, :] * _dot(act_ref[...], wd_ref[...])
    if final_norm:
        y = _rms(y) * fg_ref[...]
    o_ref[0] = y


def _ffn_call(x, o_nsa, o_lru, mod, g_nsa, g_lru, w_out, norm_g, w_gate, w_up, conv_w, conv_b, w_down, final_g,
              l, final_norm):
    nb, seq, _ = x.shape
    tm = ROW_TILE
    row = pl.BlockSpec((1, tm, D_MODEL), lambda b, t: (b, t, 0))
    half_row = pl.BlockSpec((1, tm, D_NSA), lambda b, t: (b, t, 0))
    once = dict(pipeline_mode=pl.Buffered(1))
    return pl.pallas_call(
        functools.partial(_ffn_kernel, final_norm=final_norm),
        grid=(nb, seq // tm),
        in_specs=[
            row, half_row, half_row,
            pl.BlockSpec((None, None, 6, D_MODEL), lambda b, t: (l, b, 0, 0)),
            pl.BlockSpec((None, 1, D_NSA), lambda b, t: (l, 0, 0)),
            pl.BlockSpec((None, 1, D_LRU), lambda b, t: (l, 0, 0)),
            pl.BlockSpec((None, D_MODEL, D_MODEL), lambda b, t: (l, 0, 0), **once),
            pl.BlockSpec((None, 1, D_MODEL), lambda b, t: (l, 0, 0)),
            pl.BlockSpec((None, D_MODEL, D_FF), lambda b, t: (l, 0, 0), **once),
            pl.BlockSpec((None, D_MODEL, D_FF), lambda b, t: (l, 0, 0), **once),
            pl.BlockSpec((None, FFN_CONV_W, D_FF), lambda b, t: (l, 0, 0)),
            pl.BlockSpec((None, 1, D_FF), lambda b, t: (l, 0, 0)),
            pl.BlockSpec((None, D_FF, D_MODEL), lambda b, t: (l, 0, 0), **once),
            pl.BlockSpec((1, D_MODEL), lambda b, t: (0, 0)),
        ],
        out_specs=row,
        out_shape=jax.ShapeDtypeStruct(x.shape, F32),
        scratch_shapes=[pltpu.VMEM((HIST, D_FF), F32), pltpu.VMEM((tm, D_FF), BF16)],
        compiler_params=pltpu.CompilerParams(
            dimension_semantics=("arbitrary", "arbitrary"), vmem_limit_bytes=56 * 1024 * 1024),
        name="conv_ffn",
    )(x, o_nsa, o_lru, mod, g_nsa, g_lru, w_out, norm_g, w_gate, w_up, conv_w, conv_b, w_down, final_g)


def _prep_in(w_in, gate_b):
    n_gate = N_BRANCH * HEADS
    zg0 = C_ZX
    w_p = jnp.concatenate([
        w_in[..., :zg0], w_in[..., zg0 + n_gate:], w_in[..., zg0:zg0 + n_gate],
        jnp.zeros(w_in.shape[:-1] + (LANES - n_gate,), w_in.dtype)], axis=-1).astype(BF16)
    assert w_p.shape[-1] == IN_COLS_P
    gb = jnp.pad(gate_b, ((0, 0), (0, LANES - n_gate))).reshape(DEPTH, 1, LANES)
    return w_p, gb


def _prep_cmp(cmp_pos, cmp_w1, cmp_b1, cmp_w2, cmp_b2):
    half_w = (CMP_LEN // 2) * HEAD_DIM
    pos_p = cmp_pos.reshape(DEPTH, 4, half_w)
    w1_p = cmp_w1.reshape(DEPTH, 4, half_w, CMP_HIDDEN).astype(BF16)
    b1_p = cmp_b1
    zeros = jnp.zeros((DEPTH, CMP_HIDDEN, HEAD_DIM), cmp_w2.dtype)
    w2_p = jnp.concatenate([
        jnp.concatenate([cmp_w2[:, 0], zeros], axis=-1),
        jnp.concatenate([zeros, cmp_w2[:, 1]], axis=-1)], axis=1).astype(BF16)
    b2_p = cmp_b2.reshape(DEPTH, 1, 2 * HEAD_DIM)
    return pos_p, w1_p, b1_p, w2_p, b2_p


def _prep_lru(lru_wa, lru_wx, lru_ba, lru_bx):
    per_tile = LANES // LRU_BLOCK_W
    assert per_tile == 2

    def tiles(w):
        w = w.reshape(DEPTH, LRU_BLOCKS // per_tile, per_tile, LRU_BLOCK_W, LRU_BLOCK_W)
        zero = jnp.zeros_like(w[:, :, 0])
        return jnp.concatenate([jnp.concatenate([w[:, :, 0], zero], axis=-1),
                                jnp.concatenate([zero, w[:, :, 1]], axis=-1)], axis=-2)

    w_ax = jnp.concatenate([tiles(lru_wa), tiles(lru_wx)], axis=-1).astype(BF16)
    bias_ax = jnp.stack([lru_ba, lru_bx], axis=1)
    return w_ax, bias_ax


def _attn_consts(seq):
    nblk = seq // SEL_BLOCK
    ncb_p = seq // CMP_STRIDE
    cstart = np.arange(ncb_p) * CMP_STRIDE
    js = np.arange(nblk) * SEL_BLOCK
    overlap_t = ((cstart[None, :] < js[:, None] + SEL_BLOCK) & (cstart[None, :] + CMP_LEN > js[:, None]))
    return jnp.asarray(overlap_t.astype(np.float32), BF16)


def kernel(x, c, ada_w, ada_b, mix_norm_g, ffn_norm_g, w_in, nsa_gate_b, cmp_pos, cmp_w1, cmp_b1, cmp_w2, cmp_b2, lru_conv_w, lru_conv_b, lru_wa, lru_ba, lru_wx, lru_bx, lru_lambda, nsa_out_norm_g, lru_out_norm_g, w_out, ffn_w_gate, ffn_w_up, ffn_conv_w, ffn_conv_b, ffn_w_down, final_norm_g):
    nb, seq, _ = x.shape
    mod = _mod_call(c, ada_w, ada_b)
    w_in_p, gate_b_p = _prep_in(w_in, nsa_gate_b)
    pos_p, w1_p, b1_p, w2_p, b2_p = _prep_cmp(cmp_pos, cmp_w1, cmp_b1, cmp_w2, cmp_b2)
    w_ax, bias_ax = _prep_lru(lru_wa, lru_wx, lru_ba, lru_bx)
    overlap_t = _attn_consts(seq)
    w_out_b = w_out.astype(BF16)
    wg_b, wu_b, wd_b = ffn_w_gate.astype(BF16), ffn_w_up.astype(BF16), ffn_w_down.astype(BF16)
    row3 = lambda a: a.reshape(DEPTH, 1, a.shape[-1])
    mix_g, ffn_g = row3(mix_norm_g), row3(ffn_norm_g)
    nsa_g, lru_g = row3(nsa_out_norm_g), row3(lru_out_norm_g)
    lru_cb, lam, ffn_cb = row3(lru_conv_b), row3(lru_lambda), row3(ffn_conv_b)
    final_g = final_norm_g.reshape(1, D_MODEL)

    for l in range(DEPTH):
        q, kc, vc, kx, kk, vs, vw, zx, zy, gate = _in_call(x, mod, mix_g, w_in_p, gate_b_p, l)
        kvc = _cmp_call(kc, vc, pos_p, w1_p, b1_p, w2_p, b2_p, l)
        o_nsa = _attn_call(q, kx, kk, vs, vw, kvc, gate, overlap_t)
        o_lru = _lru_call(zx, zy, lru_conv_w, lru_cb, w_ax, bias_ax, lam, l)
        x = _ffn_call(x, o_nsa, o_lru, mod, nsa_g, lru_g, w_out_b, ffn_g, wg_b, wu_b, ffn_conv_w, ffn_cb, wd_b,
                      final_g, l, l == DEPTH - 1)
    return x
```

```python
import functools

import numpy as np
import jax
import jax.numpy as jnp
from jax import lax
from jax.experimental import pallas as pl
from jax.experimental.pallas import tpu as pltpu

F32 = jnp.float32
BF16 = jnp.bfloat16

D_MODEL = 1024
DEPTH = 4
D_NSA = 512
D_LRU = 512
HEADS = 8
HEAD_DIM = 64
KV_GROUPS = 2
REP = HEADS // KV_GROUPS
N_BRANCH = 3
CMP_LEN = 32
CMP_STRIDE = 16
CMP_HIDDEN = 4 * HEAD_DIM
SEL_BLOCK = 64
SEL_TOP_K = 8
WINDOW = 512
Q_BLOCK = 128
LRU_BLOCKS = 8
LRU_BLOCK_W = D_LRU // LRU_BLOCKS
LRU_CONV_W = 4
LRU_C = 8.0
D_FF = 2816
FFN_CONV_W = 3
NORM_EPS = 1e-6
NEG_INF = -1e30
FORCE_BONUS = 1e4

KV_W = KV_GROUPS * HEAD_DIM
IN_COLS = D_NSA + 6 * KV_W + N_BRANCH * HEADS + 2 * D_LRU
LANES = 128
KEY_CHUNK = 256
SEL_CHUNK = 512
QB_PER_STEP = 2
assert SEL_CHUNK % (QB_PER_STEP * Q_BLOCK) == 0
ROW_TILE = 512
LRU_TILE = 2048
FF_CHUNK = 256
HIST = 8

C_Q = 0
C_KC = C_Q + D_NSA
C_VC = C_KC + KV_W
C_KS, C_VS, C_KW, C_VW = (C_VC + (i + 1) * KV_W for i in range(4))
C_ZX = C_VW + KV_W
C_ZY = C_ZX + D_LRU
C_GATE = C_ZY + D_LRU
IN_COLS_P = C_GATE + LANES
GATES_PER_GROUP = N_BRANCH * REP


def _dot(a, b):
    return jnp.dot(a, b, preferred_element_type=F32)


def _dot_nt(a, b):
    return lax.dot_general(a, b, (((1,), (1,)), ((), ())), preferred_element_type=F32)


def _sigmoid(x):
    return 0.5 * jnp.tanh(0.5 * x) + 0.5


def _gelu_tanh(x):
    c = np.float32(np.sqrt(2.0 / np.pi))
    return x * (0.5 * (1.0 + jnp.tanh(c * (x + 0.044715 * (x * x * x)))))


def _rms(x):
    return x * lax.rsqrt(jnp.mean(x * x, axis=-1, keepdims=True) + NORM_EPS)


def _mod_kernel(c_ref, w_ref, b_ref, o_ref):
    c = c_ref[...]
    ca = (c * _sigmoid(c)).astype(BF16)
    o_ref[...] = _dot(ca, w_ref[...].astype(BF16)) + b_ref[...]


def _mod_call(c, ada_w, ada_b):
    nb = c.shape[0]
    out = pl.pallas_call(
        _mod_kernel,
        grid=(DEPTH, 6),
        in_specs=[
            pl.BlockSpec((nb, D_MODEL), lambda l, j: (0, 0)),
            pl.BlockSpec((None, D_MODEL, D_MODEL), lambda l, j: (l, 0, j)),
            pl.BlockSpec((None, 1, D_MODEL), lambda l, j: (l, 0, j)),
        ],
        out_specs=pl.BlockSpec((None, nb, D_MODEL), lambda l, j: (l, 0, j)),
        out_shape=jax.ShapeDtypeStruct((DEPTH, nb, 6 * D_MODEL), F32),
        name="adaln_mod",
    )(c, ada_w, ada_b.reshape(DEPTH, 1, 6 * D_MODEL))
    return out.reshape(DEPTH, nb, 6, D_MODEL)


def _in_kernel(x_ref, mod_ref, g_ref, w_ref, gb_ref,
               q_ref, kc_ref, vc_ref, kx_ref, kk_ref, vs_ref, vw_ref, zx_ref, zy_ref, gate_ref, tok_ref):
    x = x_ref[0]
    tm = x.shape[0]
    lane = lax.broadcasted_iota(jnp.int32, (tm, LANES), 1)
    tok = pl.program_id(1) * tm + lax.broadcasted_iota(jnp.int32, (tm, LANES), 0)
    blk_hot = jnp.where(lane - HEAD_DIM == jnp.right_shift(tok, int(np.log2(SEL_BLOCK))), 1.0, 0.0)
    h = (_rms(x) * g_ref[...]) * (1.0 + mod_ref[1:2, :]) + mod_ref[0:1, :]
    hb = h.astype(BF16)

    def seg(start, width):
        return _dot(hb, w_ref[:, start:start + width])

    q_ref[0] = (seg(C_Q, D_NSA) * (HEAD_DIM ** -0.5)).astype(BF16)
    for out_ref, start in ((kc_ref, C_KC), (vc_ref, C_VC)):
        tok_ref[...] = seg(start, KV_W)
        for i in range(CMP_STRIDE):
            rows = tok_ref[pl.ds(i, tm // CMP_STRIDE, stride=CMP_STRIDE), :]
            for g in range(KV_GROUPS):
                c0 = (g * CMP_STRIDE + i) * HEAD_DIM
                out_ref[0, :, c0:c0 + HEAD_DIM] = rows[:, g * HEAD_DIM:(g + 1) * HEAD_DIM]
    lo = lane < HEAD_DIM
    swap = lambda a: pltpu.roll(a, HEAD_DIM, axis=1)
    ks, vs, kw, vw = (seg(c, KV_W) for c in (C_KS, C_VS, C_KW, C_VW))
    for g, (ks_g, kw_g, vs_g, vw_g) in enumerate(((ks, swap(kw), vs, swap(vw)), (swap(ks), kw, swap(vs), vw))):
        kk_ref[0, g] = jnp.where(lo, ks_g, kw_g).astype(BF16)
        kx_ref[0, g] = jnp.where(lo, ks_g, blk_hot).astype(BF16)
        vs_ref[0, g] = jnp.where(lo, vs_g, 1.0).astype(BF16)
        vw_ref[0, g] = jnp.where(lo, 1.0, vw_g).astype(BF16)
    gate = _sigmoid(seg(C_GATE, LANES) + gb_ref[...])
    for g in range(KV_GROUPS):
        gate_ref[0, g] = gate if g == 0 else pltpu.roll(gate, LANES - g * GATES_PER_GROUP, axis=1)
    zx_ref[0] = seg(C_ZX, D_LRU)
    zy_ref[0] = seg(C_ZY, D_LRU)


def _in_call(x, mod, norm_g, w_in_p, gate_b_p, l):
    nb, seq, _ = x.shape
    tm = ROW_TILE
    row = lambda width: pl.BlockSpec((1, tm, width), lambda b, t: (b, t, 0))
    grp = pl.BlockSpec((1, KV_GROUPS, tm, LANES), lambda b, t: (b, 0, t, 0))
    cmp_rows, cmp_width = tm // CMP_STRIDE, CMP_STRIDE * KV_W
    cmp_in = pl.BlockSpec((1, cmp_rows, cmp_width), lambda b, t: (b, t, 0))
    return pl.pallas_call(
        _in_kernel,
        grid=(nb, seq // tm),
        in_specs=[
            row(D_MODEL),
            pl.BlockSpec((None, None, 6, D_MODEL), lambda b, t: (l, b, 0, 0)),
            pl.BlockSpec((None, 1, D_MODEL), lambda b, t: (l, 0, 0)),
            pl.BlockSpec((None, D_MODEL, IN_COLS_P), lambda b, t: (l, 0, 0)),
            pl.BlockSpec((None, 1, LANES), lambda b, t: (l, 0, 0)),
        ],
        out_specs=[row(D_NSA), cmp_in, cmp_in, grp, grp, grp, grp, row(D_LRU), row(D_LRU), grp],
        out_shape=[
            jax.ShapeDtypeStruct((nb, seq, D_NSA), BF16),
            jax.ShapeDtypeStruct((nb, seq // CMP_STRIDE, cmp_width), F32),
            jax.ShapeDtypeStruct((nb, seq // CMP_STRIDE, cmp_width), F32),
            jax.ShapeDtypeStruct((nb, KV_GROUPS, seq, LANES), BF16),
            jax.ShapeDtypeStruct((nb, KV_GROUPS, seq, LANES), BF16),
            jax.ShapeDtypeStruct((nb, KV_GROUPS, seq, LANES), BF16),
            jax.ShapeDtypeStruct((nb, KV_GROUPS, seq, LANES), BF16),
            jax.ShapeDtypeStruct((nb, seq, D_LRU), F32),
            jax.ShapeDtypeStruct((nb, seq, D_LRU), F32),
            jax.ShapeDtypeStruct((nb, KV_GROUPS, seq, LANES), F32),
        ],
        scratch_shapes=[pltpu.VMEM((tm, KV_W), F32)],
        compiler_params=pltpu.CompilerParams(
            dimension_semantics=("arbitrary", "arbitrary"), vmem_limit_bytes=48 * 1024 * 1024),
        name="mixer_in",
    )(x, mod, norm_g, w_in_p, gate_b_p)


def _cmp_kernel(kc_ref, vc_ref, pos_ref, w1_ref, b1_ref, w2_ref, b2_ref, o_ref):
    def hidden(x_ref, i, g):
        width = pos_ref.shape[1]
        x = x_ref[0, :, g * width:(g + 1) * width]
        top = (x + pos_ref[2 * i:2 * i + 1, :]).astype(BF16)
        bot = (x + pos_ref[2 * i + 1:2 * i + 2, :]).astype(BF16)
        a = _dot(top, w1_ref[2 * i])
        b = _dot(bot, w1_ref[2 * i + 1])
        nrow = b.shape[0]
        b = pltpu.roll(b, nrow - 1, axis=0)
        return _gelu_tanh(a + b + b1_ref[i:i + 1, :])

    for g in range(KV_GROUPS):
        hg = jnp.concatenate([hidden(kc_ref, 0, g), hidden(vc_ref, 1, g)], axis=1).astype(BF16)
        o_ref[0, g] = (_dot(hg, w2_ref[...]) + b2_ref[...]).astype(BF16)


def _cmp_call(kc16, vc16, pos_p, w1_p, b1_p, w2_p, b2_p, l):
    nb, nrow, width = kc16.shape
    blk = pl.BlockSpec((1, nrow, width), lambda b: (b, 0, 0))
    return pl.pallas_call(
        _cmp_kernel,
        grid=(nb,),
        in_specs=[
            blk, blk,
            pl.BlockSpec((None, 4, width // KV_GROUPS), lambda b: (l, 0, 0)),
            pl.BlockSpec((None, 4, width // KV_GROUPS, CMP_HIDDEN), lambda b: (l, 0, 0, 0)),
            pl.BlockSpec((None, 2, CMP_HIDDEN), lambda b: (l, 0, 0)),
            pl.BlockSpec((None, 2 * CMP_HIDDEN, LANES), lambda b: (l, 0, 0)),
            pl.BlockSpec((None, 1, LANES), lambda b: (l, 0, 0)),
        ],
        out_specs=pl.BlockSpec((1, KV_GROUPS, nrow, LANES), lambda b: (b, 0, 0, 0)),
        out_shape=jax.ShapeDtypeStruct((nb, KV_GROUPS, nrow, LANES), BF16),
        compiler_params=pltpu.CompilerParams(
            dimension_semantics=("arbitrary",), vmem_limit_bytes=48 * 1024 * 1024),
        name="nsa_compress",
    )(kc16, vc16, pos_p, w1_p, b1_p, w2_p, b2_p)


def _attn_body(n_sel, g, j, q_ref, kx_ref, kk_ref, vs_ref, vw_ref, kvc_ref, gate_ref, ot_ref, o_ref):
    qb = QB_PER_STEP * pl.program_id(1) + j
    rows_j = slice(j * Q_BLOCK, (j + 1) * Q_BLOCK)
    nq = Q_BLOCK
    nrow = REP * nq
    q = q_ref[0, rows_j, g * REP * HEAD_DIM:(g + 1) * REP * HEAD_DIM]
    qs = jnp.concatenate([q[:, r * HEAD_DIM:(r + 1) * HEAD_DIM] for r in range(REP)], axis=0)
    zpad = jnp.zeros_like(qs)
    q_lo = jnp.concatenate([qs, zpad], axis=1)
    q_hi = jnp.concatenate([zpad, qs], axis=1)

    def tok_rows(width):
        return qb * nq + lax.broadcasted_iota(jnp.int32, (nq, width), 0)

    lane_row = lax.broadcasted_iota(jnp.int32, (nrow, LANES), 1)

    def lane_max(x):
        return functools.reduce(jnp.maximum, [x[:, i:i + LANES] for i in range(0, x.shape[1], LANES)])

    def rep_lanes(x, width):
        return jnp.concatenate([x] * (width // LANES), axis=1)

    def row_bcast(x):
        return jnp.broadcast_to(x, (x.shape[0], LANES))

    gt = gate_ref[0, g, rows_j, :]

    def gate_rows(br):
        return jnp.concatenate(
            [jnp.broadcast_to(gt[:, N_BRANCH * r + br:N_BRANCH * r + br + 1], (nq, LANES)) for r in range(REP)],
            axis=0)

    kvc = kvc_ref[0, g]
    ncb = kvc.shape[0]
    sc = _dot_nt(q_lo, kvc).reshape(REP, nq, ncb)
    cidx = lax.broadcasted_iota(jnp.int32, (nq, ncb), 1)
    mask_c = (cidx * CMP_STRIDE + (CMP_LEN - 1)) <= tok_rows(ncb)
    sc = jnp.where(mask_c[None], sc, NEG_INF)
    mc = jnp.max(sc, axis=-1, keepdims=True)
    pc = jnp.where(mask_c[None], jnp.exp(sc - mc), 0.0)
    lc = jnp.sum(pc, axis=-1, keepdims=True)
    pc = pc / jnp.where(lc > 0.0, lc, 1.0)
    o_cmp = _dot(pc.reshape(nrow, ncb).astype(BF16), kvc)

    psum = pc[0] + pc[1] + pc[2] + pc[3]
    p_hi = psum.astype(BF16)
    p_lo = (psum - p_hi.astype(F32)).astype(BF16)
    ot = ot_ref[...]
    imp = _dot_nt(ot, p_hi) + _dot_nt(ot, p_lo)
    nblk = imp.shape[0]
    jj = lax.broadcasted_iota(jnp.int32, (nblk, nq), 0)
    lane = lax.broadcasted_iota(jnp.int32, (nblk, nq), 1)
    qblk = jnp.right_shift(qb * nq + lane, int(np.log2(SEL_BLOCK)))
    valid = jj <= qblk
    forced = (jj == 0) | (jj == qblk) | (jj == qblk - 1)
    imp = jnp.where(valid, imp + jnp.where(forced, FORCE_BONUS, 0.0), NEG_INF)
    rank = jnp.zeros((nblk, nq), F32)
    for i in range(nblk):
        row = imp[i:i + 1, :]
        beats = (row > imp) | ((row == imp) & (jj > i))
        rank = rank + jnp.where(beats, 1.0, 0.0)
    sel_t = jnp.where(rank < float(SEL_TOP_K), 1.0, 0.0)
    sel_t = jnp.concatenate(
        [sel_t, jnp.ones((HEAD_DIM - nblk, nq), F32), jnp.zeros((LANES - HEAD_DIM, nq), F32)], axis=0)
    selb = ((sel_t.T[:, :HEAD_DIM] - 1.0) * -NEG_INF).astype(BF16)
    q_sel = jnp.concatenate([qs, jnp.concatenate([selb] * REP, axis=0)], axis=1)

    n_win = WINDOW // KEY_CHUNK + (0 if n_sel == 1 else 1)
    diag = qb // (KEY_CHUNK // nq)
    k0_win, ok_win = [], []
    for i in range(n_win):
        cw = diag - (n_win - 1) + i
        k0_win.append(pl.multiple_of(jnp.maximum(cw, 0) * KEY_CHUNK, KEY_CHUNK))
        kpos = cw * KEY_CHUNK + lax.broadcasted_iota(jnp.int32, (nq, KEY_CHUNK), 1)
        tq = tok_rows(KEY_CHUNK)
        ok_win.append((kpos <= tq) & (kpos > tq - WINDOW) & (kpos >= 0))
    win_parts = 2 if n_sel == 1 else 1
    wrows = nrow // win_parts
    acc_parts_w = []
    for hh in range(win_parts):
        qh = q_hi[hh * wrows:(hh + 1) * wrows]
        s_win = []
        for k0, ok in zip(k0_win, ok_win):
            s = _dot_nt(qh, kk_ref[0, g, pl.ds(k0, KEY_CHUNK), :]).reshape(wrows // nq, nq, KEY_CHUNK)
            s_win.append(jnp.where(ok[None], s, NEG_INF).reshape(wrows, KEY_CHUNK))
        m_w = row_bcast(jnp.max(functools.reduce(jnp.maximum, [lane_max(s) for s in s_win]), axis=1, keepdims=True))
        acc_parts_w.append(functools.reduce(jnp.add, [
            _dot(jnp.exp(s - rep_lanes(m_w, KEY_CHUNK)).astype(BF16), vw_ref[0, g, pl.ds(k0, KEY_CHUNK), :])
            for s, k0 in zip(s_win, k0_win)]))
    acc_w = jnp.concatenate(acc_parts_w, axis=0)
    o_hi = gate_rows(2) * acc_w / pltpu.roll(acc_w, HEAD_DIM, axis=1) + gate_rows(0) * o_cmp

    sel_parts = 1
    part = nrow // sel_parts
    acc_parts = []
    for hh in range(sel_parts):
        qh = q_sel[hh * part:(hh + 1) * part]
        s_sel = []
        for c in range(n_sel):
            s = _dot_nt(qh, kx_ref[0, g, c * SEL_CHUNK:(c + 1) * SEL_CHUNK, :])
            if c == n_sel - 1:
                kpos = c * SEL_CHUNK + lax.broadcasted_iota(jnp.int32, (nq, SEL_CHUNK), 1)
                ok = kpos <= tok_rows(SEL_CHUNK)
                s = jnp.where(ok[None], s.reshape(part // nq, nq, SEL_CHUNK), NEG_INF).reshape(part, SEL_CHUNK)
            s_sel.append(s)
        m_s = row_bcast(jnp.max(functools.reduce(jnp.maximum, [lane_max(s) for s in s_sel]), axis=1, keepdims=True))
        acc_parts.append(functools.reduce(jnp.add, [
            _dot(jnp.exp(s - rep_lanes(m_s, SEL_CHUNK)).astype(BF16),
                 vs_ref[0, g, c * SEL_CHUNK:(c + 1) * SEL_CHUNK, :])
            for c, s in enumerate(s_sel)]))
    acc_s = jnp.concatenate(acc_parts, axis=0)
    o_lo = gate_rows(1) * acc_s / pltpu.roll(acc_s, HEAD_DIM, axis=1)

    t = jnp.where(lane_row < HEAD_DIM, o_lo, o_hi)
    u = t + pltpu.roll(t, HEAD_DIM, axis=1)
    lane_q = lax.broadcasted_iota(jnp.int32, (nq, LANES), 1)
    o_ref[0, rows_j, g * REP * HEAD_DIM:(g + 1) * REP * HEAD_DIM] = jnp.concatenate(
        [jnp.where(lane_q < HEAD_DIM, u[(2 * i) * nq:(2 * i + 1) * nq], u[(2 * i + 1) * nq:(2 * i + 2) * nq])
         for i in range(REP // 2)], axis=1)


def _attn_kernel(*refs):
    chunks = pl.program_id(1) // (SEL_CHUNK // (QB_PER_STEP * Q_BLOCK)) + 1

    def all_blocks(n_sel):
        for j in range(QB_PER_STEP):
            for g in range(KV_GROUPS):
                _attn_body(n_sel, g, j, *refs)

    for n_sel in range(1, refs[1].shape[2] // SEL_CHUNK + 1):
        pl.when(chunks == n_sel)(functools.partial(all_blocks, n_sel))


def _attn_call(q, kx, kk, vs, vw, kvc, gate, overlap_t):
    nb, seq, _ = q.shape
    nq = seq // Q_BLOCK
    ncb = kvc.shape[2]
    per_b = lambda rows: pl.BlockSpec((1, KV_GROUPS, rows, LANES), lambda b, i: (b, 0, 0, 0))
    return pl.pallas_call(
        _attn_kernel,
        grid=(nb, nq // QB_PER_STEP),
        in_specs=[
            pl.BlockSpec((1, QB_PER_STEP * Q_BLOCK, D_NSA), lambda b, i: (b, i, 0)),
            per_b(seq), per_b(seq), per_b(seq), per_b(seq), per_b(ncb),
            pl.BlockSpec((1, KV_GROUPS, QB_PER_STEP * Q_BLOCK, LANES), lambda b, i: (b, 0, i, 0)),
            pl.BlockSpec(overlap_t.shape, lambda b, i: (0, 0)),
        ],
        out_specs=pl.BlockSpec((1, QB_PER_STEP * Q_BLOCK, D_NSA), lambda b, i: (b, i, 0)),
        out_shape=jax.ShapeDtypeStruct((nb, seq, D_NSA), F32),
        compiler_params=pltpu.CompilerParams(
            dimension_semantics=("arbitrary", "arbitrary"),
            vmem_limit_bytes=48 * 1024 * 1024),
        name="nsa_attention",
    )(q, kx, kk, vs, vw, kvc, gate, overlap_t)


def _lru_kernel(zx_ref, zy_ref, cw_ref, cb_ref, w_ref, bias_ref, lam_ref, o_ref, hist_ref, h_ref):
    t = pl.program_id(2)
    tl = zx_ref.shape[1]
    ng = tl // HIST
    taps = LRU_CONV_W - 1

    @pl.when(t == 0)
    def _():
        hist_ref[...] = jnp.zeros(hist_ref.shape, F32)
        h_ref[...] = jnp.zeros(h_ref.shape, F32)

    xs = [zx_ref[0, pl.ds(j, HIST, stride=ng), :] for j in range(ng)]
    sub = lax.broadcasted_iota(jnp.int32, (HIST, LANES), 0)
    xe = [jnp.where(sub == 0, hist_ref[k:k + 1, :], pltpu.roll(xs[ng - taps + k], 1, axis=0))
          for k in range(taps)] + xs
    for k in range(taps):
        hist_ref[k:k + 1, :] = xs[ng - taps + k][HIST - 1:HIST, :]
    cw = [jnp.broadcast_to(cw_ref[k:k + 1, :], (HIST, LANES)) for k in range(LRU_CONV_W)]
    cb = jnp.broadcast_to(cb_ref[...], (HIST, LANES))
    xc = jnp.concatenate(
        [cb + functools.reduce(jnp.add, [cw[k] * xe[j + k] for k in range(LRU_CONV_W)]) for j in range(ng)], axis=0)

    y = _dot(xc.astype(BF16), w_ref[...])
    r = _sigmoid(y[:, :LANES] + bias_ref[0:1, :])
    i = _sigmoid(y[:, LANES:] + bias_ref[1:2, :])
    lam = lam_ref[...]
    log_sig = -(jnp.maximum(-lam, 0.0) + jnp.log1p(jnp.exp(-jnp.abs(lam))))
    log_a = r * (LRU_C * log_sig)
    a = jnp.exp(log_a)
    u = jnp.sqrt(-jnp.tanh(log_a) * (a * a + 1.0)) * (i * xc)

    hs, ps = [u[:HIST]], [a[:HIST]]
    for j in range(1, ng):
        aj = a[j * HIST:(j + 1) * HIST]
        hs.append(aj * hs[-1] + u[j * HIST:(j + 1) * HIST])
        ps.append(aj * ps[-1])
    c = h_ref[0:1, :]
    carry = [c]
    for rr in range(HIST - 1):
        c = ps[-1][rr:rr + 1, :] * c + hs[-1][rr:rr + 1, :]
        carry.append(c)
    h_ref[0:1, :] = ps[-1][HIST - 1:HIST, :] * c + hs[-1][HIST - 1:HIST, :]
    carry = jnp.concatenate(carry, axis=0)
    for j in range(ng):
        h = hs[j] + ps[j] * carry
        o_ref[0, pl.ds(j, HIST, stride=ng), :] = h * _gelu_tanh(zy_ref[0, pl.ds(j, HIST, stride=ng), :])


def _lru_call(zx, zy, conv_w, conv_b, w_ax, bias_ax, lam, l):
    nb, seq, _ = zx.shape
    tl = LRU_TILE
    row = pl.BlockSpec((1, tl, LANES), lambda b, c, t: (b, t, c))
    par = lambda rows: pl.BlockSpec((None, rows, LANES), lambda b, c, t: (l, 0, c))
    return pl.pallas_call(
        _lru_kernel,
        grid=(nb, D_LRU // LANES, seq // tl),
        in_specs=[
            row, row, par(LRU_CONV_W), par(1),
            pl.BlockSpec((None, None, LANES, 2 * LANES), lambda b, c, t: (l, c, 0, 0)),
            par(2), par(1),
        ],
        out_specs=row,
        out_shape=jax.ShapeDtypeStruct((nb, seq, D_LRU), F32),
        scratch_shapes=[pltpu.VMEM((HIST, LANES), F32), pltpu.VMEM((HIST, LANES), F32)],
        compiler_params=pltpu.CompilerParams(dimension_semantics=("arbitrary", "arbitrary", "arbitrary")),
        name="rglru",
    )(zx, zy, conv_w, conv_b, w_ax, bias_ax, lam)


def _ffn_kernel(x_ref, on_ref, ol_ref, mod_ref, gn_ref, gl_ref, wo_ref, g_ref, wg_ref, wu_ref, cw_ref, cb_ref,
                wd_ref, fg_ref, o_ref, hist_ref, act_ref, *, final_norm):
    t = pl.program_id(1)
    tm = x_ref.shape[1]

    @pl.when(t == 0)
    def _():
        hist_ref[...] = jnp.zeros(hist_ref.shape, F32)

    a = (_rms(on_ref[0]) * gn_ref[...]).astype(BF16)
    b = (_rms(ol_ref[0]) * gl_ref[...]).astype(BF16)
    x = x_ref[0] + mod_ref[2:3, :] * (_dot(a, wo_ref[:D_NSA, :]) + _dot(b, wo_ref[D_NSA:, :]))
    h = (_rms(x) * g_ref[...]) * (1.0 + mod_ref[4:5, :]) + mod_ref[3:4, :]
    hb = h.astype(BF16)
    for j in range(D_FF // FF_CHUNK):
        cs = slice(j * FF_CHUNK, (j + 1) * FF_CHUNK)
        gp = _dot(hb, wg_ref[:, cs])
        ge = jnp.concatenate([hist_ref[:, cs], gp], axis=0)
        hist_ref[:, cs] = gp[tm - HIST:, :]
        gate = cb_ref[:, cs] + gp * cw_ref[FFN_CONV_W - 1:FFN_CONV_W, cs]
        for k in range(FFN_CONV_W - 1):
            off = HIST - (FFN_CONV_W - 1) + k
            gate = gate + ge[off:off + tm, :] * cw_ref[k:k + 1, cs]
        act = (gate * _sigmoid(gate)) * _dot(hb, wu_ref[:, cs])
        act_ref[:, cs] = act.astype(BF16)
    y = x + mod_ref[5:6, :] * _dot(act_ref[...], wd_ref[...])
    if final_norm:
        y = _rms(y) * fg_ref[...]
    o_ref[0] = y


def _ffn_call(x, o_nsa, o_lru, mod, g_nsa, g_lru, w_out, norm_g, w_gate, w_up, conv_w, conv_b, w_down, final_g,
              l, final_norm):
    nb, seq, _ = x.shape
    tm = ROW_TILE
    row = pl.BlockSpec((1, tm, D_MODEL), lambda b, t: (b, t, 0))
    half_row = pl.BlockSpec((1, tm, D_NSA), lambda b, t: (b, t, 0))
    once = dict(pipeline_mode=pl.Buffered(1))
    return pl.pallas_call(
        functools.partial(_ffn_kernel, final_norm=final_norm),
        grid=(nb, seq // tm),
        in_specs=[
            row, half_row, half_row,
            pl.BlockSpec((None, None, 6, D_MODEL), lambda b, t: (l, b, 0, 0)),
            pl.BlockSpec((None, 1, D_NSA), lambda b, t: (l, 0, 0)),
            pl.BlockSpec((None, 1, D_LRU), lambda b, t: (l, 0, 0)),
            pl.BlockSpec((None, D_MODEL, D_MODEL), lambda b, t: (l, 0, 0), **once),
            pl.BlockSpec((None, 1, D_MODEL), lambda b, t: (l, 0, 0)),
            pl.BlockSpec((None, D_MODEL, D_FF), lambda b, t: (l, 0, 0), **once),
            pl.BlockSpec((None, D_MODEL, D_FF), lambda b, t: (l, 0, 0), **once),
            pl.BlockSpec((None, FFN_CONV_W, D_FF), lambda b, t: (l, 0, 0)),
            pl.BlockSpec((None, 1, D_FF), lambda b, t: (l, 0, 0)),
            pl.BlockSpec((None, D_FF, D_MODEL), lambda b, t: (l, 0, 0), **once),
            pl.BlockSpec((1, D_MODEL), lambda b, t: (0, 0)),
        ],
        out_specs=row,
        out_shape=jax.ShapeDtypeStruct(x.shape, F32),
        scratch_shapes=[pltpu.VMEM((HIST, D_FF), F32), pltpu.VMEM((tm, D_FF), BF16)],
        compiler_params=pltpu.CompilerParams(
            dimension_semantics=("arbitrary", "arbitrary"), vmem_limit_bytes=56 * 1024 * 1024),
        name="conv_ffn",
    )(x, o_nsa, o_lru, mod, g_nsa, g_lru, w_out, norm_g, w_gate, w_up, conv_w, conv_b, w_down, final_g)


def _prep_in(w_in, gate_b):
    n_gate = N_BRANCH * HEADS
    zg0 = C_ZX
    w_p = jnp.concatenate([
        w_in[..., :zg0], w_in[..., zg0 + n_gate:], w_in[..., zg0:zg0 + n_gate],
        jnp.zeros(w_in.shape[:-1] + (LANES - n_gate,), w_in.dtype)], axis=-1).astype(BF16)
    assert w_p.shape[-1] == IN_COLS_P
    gb = jnp.pad(gate_b, ((0, 0), (0, LANES - n_gate))).reshape(DEPTH, 1, LANES)
    return w_p, gb


def _prep_cmp(cmp_pos, cmp_w1, cmp_b1, cmp_w2, cmp_b2):
    half_w = (CMP_LEN // 2) * HEAD_DIM
    pos_p = cmp_pos.reshape(DEPTH, 4, half_w)
    w1_p = cmp_w1.reshape(DEPTH, 4, half_w, CMP_HIDDEN).astype(BF16)
    b1_p = cmp_b1
    zeros = jnp.zeros((DEPTH, CMP_HIDDEN, HEAD_DIM), cmp_w2.dtype)
    w2_p = jnp.concatenate([
        jnp.concatenate([cmp_w2[:, 0], zeros], axis=-1),
        jnp.concatenate([zeros, cmp_w2[:, 1]], axis=-1)], axis=1).astype(BF16)
    b2_p = cmp_b2.reshape(DEPTH, 1, 2 * HEAD_DIM)
    return pos_p, w1_p, b1_p, w2_p, b2_p


def _prep_lru(lru_wa, lru_wx, lru_ba, lru_bx):
    per_tile = LANES // LRU_BLOCK_W
    assert per_tile == 2

    def tiles(w):
        w = w.reshape(DEPTH, LRU_BLOCKS // per_tile, per_tile, LRU_BLOCK_W, LRU_BLOCK_W)
        zero = jnp.zeros_like(w[:, :, 0])
        return jnp.concatenate([jnp.concatenate([w[:, :, 0], zero], axis=-1),
                                jnp.concatenate([zero, w[:, :, 1]], axis=-1)], axis=-2)

    w_ax = jnp.concatenate([tiles(lru_wa), tiles(lru_wx)], axis=-1).astype(BF16)
    bias_ax = jnp.stack([lru_ba, lru_bx], axis=1)
    return w_ax, bias_ax


def _attn_consts(seq):
    nblk = seq // SEL_BLOCK
    ncb_p = seq // CMP_STRIDE
    cstart = np.arange(ncb_p) * CMP_STRIDE
    js = np.arange(nblk) * SEL_BLOCK
    overlap_t = ((cstart[None, :] < js[:, None] + SEL_BLOCK) & (cstart[None, :] + CMP_LEN > js[:, None]))
    return jnp.asarray(overlap_t.astype(np.float32), BF16)


def kernel(x, c, ada_w, ada_b, mix_norm_g, ffn_norm_g, w_in, nsa_gate_b, cmp_pos, cmp_w1, cmp_b1, cmp_w2, cmp_b2, lru_conv_w, lru_conv_b, lru_wa, lru_ba, lru_wx, lru_bx, lru_lambda, nsa_out_norm_g, lru_out_norm_g, w_out, ffn_w_gate, ffn_w_up, ffn_conv_w, ffn_conv_b, ffn_w_down, final_norm_g):
    nb, seq, _ = x.shape
    mod = _mod_call(c, ada_w, ada_b)
    w_in_p, gate_b_p = _prep_in(w_in, nsa_gate_b)
    pos_p, w1_p, b1_p, w2_p, b2_p = _prep_cmp(cmp_pos, cmp_w1, cmp_b1, cmp_w2, cmp_b2)
    w_ax, bias_ax = _prep_lru(lru_wa, lru_wx, lru_ba, lru_bx)
    overlap_t = _attn_consts(seq)
    w_out_b = w_out.astype(BF16)
    wg_b, wu_b, wd_b = ffn_w_gate.astype(BF16), ffn_w_up.astype(BF16), ffn_w_down.astype(BF16)
    row3 = lambda a: a.reshape(DEPTH, 1, a.shape[-1])
    mix_g, ffn_g = row3(mix_norm_g), row3(ffn_norm_g)
    nsa_g, lru_g = row3(nsa_out_norm_g), row3(lru_out_norm_g)
    lru_cb, lam, ffn_cb = row3(lru_conv_b), row3(lru_lambda), row3(ffn_conv_b)
    final_g = final_norm_g.reshape(1, D_MODEL)

    for l in range(DEPTH):
        q, kc, vc, kx, kk, vs, vw, zx, zy, gate = _in_call(x, mod, mix_g, w_in_p, gate_b_p, l)
        kvc = _cmp_call(kc, vc, pos_p, w1_p, b1_p, w2_p, b2_p, l)
        o_nsa = _attn_call(q, kx, kk, vs, vw, kvc, gate, overlap_t)
        o_lru = _lru_call(zx, zy, lru_conv_w, lru_cb, w_ax, bias_ax, lam, l)
        x = _ffn_call(x, o_nsa, o_lru, mod, nsa_g, lru_g, w_out_b, ffn_g, wg_b, wu_b, ffn_conv_w, ffn_cb, wd_b,
                      final_g, l, l == DEPTH - 1)
    return x
```
